```python
import math
import jax
import jax.numpy as jnp
from jax import lax
import numpy as np

D_MODEL = 4096
BATCH = 4
SEQ = 4096
DEPTH = 2

N_BRANCH = 4
BRANCH_WIDTH = D_MODEL // 4
CONV_WIDTH = 4
NORM_EPS = 1e-6
DN_HEAD_DIM = 128
DN_HEADS = BRANCH_WIDTH // DN_HEAD_DIM
DN_CHUNK = 64
LRU_WIDTH = BRANCH_WIDTH
LRU_BLOCKS = 8
LRU_BLOCK = LRU_WIDTH // LRU_BLOCKS
LRU_C = 8.0
SSM_WIDTH = BRANCH_WIDTH
SSM_GROUP = 16
SSM_GROUPS = SSM_WIDTH // SSM_GROUP
SSM_STATE = 64
MEM_LEN = 256
MEM_HEADS = 4
MEM_HEAD_DIM = BRANCH_WIDTH // MEM_HEADS
GATE_RANK = 256
IN_SIZES = (BRANCH_WIDTH, BRANCH_WIDTH, BRANCH_WIDTH, BRANCH_WIDTH, DN_HEADS, DN_HEADS,
            LRU_WIDTH, LRU_WIDTH, SSM_WIDTH, SSM_WIDTH, BRANCH_WIDTH, BRANCH_WIDTH, GATE_RANK)
D_IN = sum(IN_SIZES)

kernel_name = "hybrid_gated_parallel_mixers"


def _rms_norm(x, w):
    xf = x.astype(jnp.float32)
    var = jnp.mean(xf * xf, axis=-1, keepdims=True)
    return (xf * lax.rsqrt(var + NORM_EPS) * w.astype(jnp.float32)).astype(x.dtype)


def _l2_norm(x):
    return x * lax.rsqrt(jnp.sum(x * x, axis=-1, keepdims=True) + NORM_EPS)


def _causal_conv(x, w):
    s = x.shape[1]
    xp = jnp.pad(x, ((0, 0), (CONV_WIDTH - 1, 0), (0, 0)))
    y = xp[:, 0:s] * w[0]
    for j in range(1, CONV_WIDTH):
        y = y + xp[:, j:j + s] * w[j]
    return y


def _linear_combine(e1, e2):
    a1, b1 = e1
    a2, b2 = e2
    return a1 * a2, a2 * b1 + b2


def _complex_combine(e1, e2):
    a1r, a1i, b1r, b1i = e1
    a2r, a2i, b2r, b2i = e2
    ar = a2r * a1r - a2i * a1i
    ai = a2r * a1i + a2i * a1r
    br = a2r * b1r - a2i * b1i + b2r
    bi = a2r * b1i + a2i * b1r + b2i
    return ar, ai, br, bi


def _chunk_gated_delta_rule(q, k, v, g, beta):
    f32 = jnp.float32
    b, s, h, dk = q.shape
    dv = v.shape[-1]
    n = s // DN_CHUNK

    def to_chunks(t):
        return t.astype(f32).reshape(b, n, DN_CHUNK, h, -1).transpose(0, 3, 1, 2, 4)

    q = to_chunks(q) * (dk ** -0.5)
    k = to_chunks(k)
    v = to_chunks(v)
    beta = to_chunks(beta[..., None])
    g = jnp.cumsum(to_chunks(g[..., None])[..., 0], axis=-1)
    idx = jnp.arange(DN_CHUNK)
    causal = idx[:, None] >= idx[None, :]
    strict = idx[:, None] > idx[None, :]
    decay = jnp.exp(jnp.where(causal, g[..., :, None] - g[..., None, :], -jnp.inf))
    k_beta = k * beta
    v_beta = v * beta
    kk = jnp.einsum("bhncd,bhnjd->bhncj", k_beta, k) * decay
    lower = jnp.where(strict, kk, 0.0) + jnp.eye(DN_CHUNK, dtype=f32)
    rhs = jnp.concatenate([v_beta, k_beta * jnp.exp(g)[..., None]], axis=-1)
    sol = lax.linalg.triangular_solve(lower, rhs, left_side=True, lower=True, unit_diagonal=True)
    u, w = sol[..., :dv], sol[..., dv:]
    qk = jnp.where(causal, jnp.einsum("bhncd,bhnjd->bhncj", q, k) * decay, 0.0)
    g_last = g[..., -1]
    k_dec = k * jnp.exp(g_last[..., None] - g)[..., None]
    q_dec = q * jnp.exp(g)[..., None]

    def step(state, xs):
        q_c, qk_c, u_c, w_c, k_c, gl_c = xs
        v_new = u_c - jnp.einsum("bhcd,bhde->bhce", w_c, state)
        out = jnp.einsum("bhcd,bhde->bhce", q_c, state) + jnp.einsum("bhcj,bhje->bhce", qk_c, v_new)
        state = state * jnp.exp(gl_c)[..., None, None] + jnp.einsum("bhcd,bhce->bhde", k_c, v_new)
        return state, out

    xs = tuple(jnp.moveaxis(t, 2, 0) for t in (q_dec, qk, u, w, k_dec, g_last))
    state0 = jnp.zeros((b, h, dk, dv), f32)
    _, out = lax.scan(step, state0, xs)
    return out.transpose(1, 0, 3, 2, 4).reshape(b, s, h, dv)


def _deltanet_branch(q, k, v, z, beta_logit, alpha_logit, conv_w, a_log, dt_bias, norm_w):
    f32 = jnp.float32
    b, s, _ = q.shape
    qkv = jax.nn.silu(_causal_conv(jnp.concatenate([q, k, v], axis=-1).astype(f32), conv_w.astype(f32)))
    q, k, v = jnp.split(qkv, 3, axis=-1)
    q = _l2_norm(q.reshape(b, s, DN_HEADS, DN_HEAD_DIM))
    k = _l2_norm(k.reshape(b, s, DN_HEADS, DN_HEAD_DIM))
    v = v.reshape(b, s, DN_HEADS, DN_HEAD_DIM)
    beta = jax.nn.sigmoid(beta_logit.astype(f32))
    g = -jnp.exp(a_log.astype(f32)) * jax.nn.softplus(alpha_logit.astype(f32) + dt_bias.astype(f32))
    o = _chunk_gated_delta_rule(q, k, v, g, beta)
    o = _rms_norm(o, norm_w) * jax.nn.silu(z.astype(f32)).reshape(b, s, DN_HEADS, DN_HEAD_DIM)
    return o.reshape(b, s, BRANCH_WIDTH)


def _rglru_branch(xb, z, conv_w, conv_b, w_r, b_r, w_i, b_i, lam):
    f32 = jnp.float32
    b, s, _ = xb.shape
    xc = _causal_conv(xb.astype(f32), conv_w.astype(f32)) + conv_b.astype(f32)
    blocks = xc.reshape(b, s, LRU_BLOCKS, LRU_BLOCK)
    r = jax.nn.sigmoid(jnp.einsum("bsni,nij->bsnj", blocks, w_r.astype(f32)).reshape(b, s, LRU_WIDTH) + b_r.astype(f32))
    i = jax.nn.sigmoid(jnp.einsum("bsni,nij->bsnj", blocks, w_i.astype(f32)).reshape(b, s, LRU_WIDTH) + b_i.astype(f32))
    log_a = -LRU_C * r * jax.nn.softplus(-lam.astype(f32))
    a = jnp.exp(log_a)
    inp = jnp.sqrt(-jnp.expm1(2.0 * log_a)) * (i * xc)
    _, hs = lax.associative_scan(_linear_combine, (a, inp), axis=1)
    return hs * jax.nn.silu(z.astype(f32))


def _s5_branch(u, z, log_dt, a_re, a_im, b_re, b_im, c_re, c_im, d_skip, w_glu, b_glu):
    f32 = jnp.float32
    b, s, _ = u.shape
    ug = u.astype(f32).reshape(b, s, SSM_GROUPS, SSM_GROUP)
    a_re = a_re.astype(f32)
    a_im = a_im.astype(f32)
    b_re = b_re.astype(f32)
    b_im = b_im.astype(f32)
    dt = jnp.exp(log_dt.astype(f32))[:, None]
    mag = jnp.exp(dt * a_re)
    ab_re = mag * jnp.cos(dt * a_im)
    ab_im = mag * jnp.sin(dt * a_im)
    den = a_re * a_re + a_im * a_im
    f_re = ((ab_re - 1.0) * a_re + ab_im * a_im) / den
    f_im = (ab_im * a_re - (ab_re - 1.0) * a_im) / den
    bb_re = f_re[..., None] * b_re - f_im[..., None] * b_im
    bb_im = f_re[..., None] * b_im + f_im[..., None] * b_re
    bu_re = jnp.einsum("bsgc,gnc->bsgn", ug, bb_re)
    bu_im = jnp.einsum("bsgc,gnc->bsgn", ug, bb_im)
    shape = bu_re.shape
    _, _, x_re, x_im = lax.associative_scan(
        _complex_combine,
        (jnp.broadcast_to(ab_re, shape), jnp.broadcast_to(ab_im, shape), bu_re, bu_im),
        axis=1)
    y = (jnp.einsum("bsgn,gcn->bsgc", x_re, c_re.astype(f32))
         - jnp.einsum("bsgn,gcn->bsgc", x_im, c_im.astype(f32))
         + d_skip.astype(f32) * ug)
    y = jax.nn.gelu(y.reshape(b, s, SSM_WIDTH))
    val, gate = jnp.split(y @ w_glu.astype(f32) + b_glu.astype(f32), 2, axis=-1)
    return val * jax.nn.sigmoid(gate) * jax.nn.silu(z.astype(f32))


def _memory_branch(q, z, mem, mem_norm_w, w_kv):
    f32 = jnp.float32
    b, s, _ = q.shape
    m_len = mem.shape[1]
    m = _rms_norm(mem, mem_norm_w)
    k, v = jnp.split(m @ w_kv, 2, axis=-1)
    k = k.astype(f32).reshape(b, m_len, MEM_HEADS, MEM_HEAD_DIM)
    v = v.astype(f32).reshape(b, m_len, MEM_HEADS, MEM_HEAD_DIM)
    qh = q.astype(f32).reshape(b, s, MEM_HEADS, MEM_HEAD_DIM)
    scores = jnp.einsum("bshd,bmhd->bhsm", qh, k) * (MEM_HEAD_DIM ** -0.5)
    p = jax.nn.softmax(scores, axis=-1)
    o = jnp.einsum("bhsm,bmhd->bshd", p, v).reshape(b, s, BRANCH_WIDTH)
    return o * jax.nn.silu(z.astype(f32))


def setup_inputs(seed: int = 0) -> dict:
    key = jax.random.key(seed)
    ks = jax.random.split(key, 32)
    f32 = jnp.float32

    def nrm(k, shape, scale):
        return jax.random.normal(k, shape, f32) * scale

    def unif(k, shape, lo, hi):
        return jax.random.uniform(k, shape, f32, lo, hi)

    x = nrm(ks[0], (BATCH, SEQ, D_MODEL), 1.0)
    mem = nrm(ks[1], (BATCH, MEM_LEN, D_MODEL), 1.0)
    norm_w = 1.0 + nrm(ks[2], (DEPTH, D_MODEL), 0.02)
    w_in = nrm(ks[3], (DEPTH, D_MODEL, D_IN), D_MODEL ** -0.5)
    dn_conv_w = nrm(ks[4], (DEPTH, CONV_WIDTH, 3 * BRANCH_WIDTH), CONV_WIDTH ** -0.5)
    dn_a_log = jnp.log(unif(ks[5], (DEPTH, DN_HEADS), 1.0, 16.0))
    dn_dt = jnp.exp(unif(ks[6], (DEPTH, DN_HEADS), math.log(1e-3), math.log(1e-1)))
    dn_dt_bias = dn_dt + jnp.log(-jnp.expm1(-dn_dt))
    dn_norm_w = 1.0 + nrm(ks[7], (DEPTH, DN_HEAD_DIM), 0.02)
    lru_conv_w = nrm(ks[8], (DEPTH, CONV_WIDTH, LRU_WIDTH), CONV_WIDTH ** -0.5)
    lru_conv_b = nrm(ks[9], (DEPTH, LRU_WIDTH), 0.01)
    lru_w_r = nrm(ks[10], (DEPTH, LRU_BLOCKS, LRU_BLOCK, LRU_BLOCK), LRU_BLOCK ** -0.5)
    lru_b_r = nrm(ks[11], (DEPTH, LRU_WIDTH), 0.01)
    lru_w_i = nrm(ks[12], (DEPTH, LRU_BLOCKS, LRU_BLOCK, LRU_BLOCK), LRU_BLOCK ** -0.5)
    lru_b_i = nrm(ks[13], (DEPTH, LRU_WIDTH), 0.01)
    a_pow = unif(ks[14], (DEPTH, LRU_WIDTH), 0.9, 0.999)
    a0 = a_pow ** (1.0 / LRU_C)
    lru_lambda = jnp.log(a0) - jnp.log1p(-a0)
    ssm_log_dt = unif(ks[15], (DEPTH, SSM_GROUPS), math.log(1e-3), math.log(1e-1))
    ssm_a_re = -0.5 + nrm(ks[16], (DEPTH, SSM_GROUPS, SSM_STATE), 0.01)
    ssm_a_im = math.pi * jnp.arange(SSM_STATE, dtype=f32) + nrm(ks[17], (DEPTH, SSM_GROUPS, SSM_STATE), 0.01)
    ssm_b_re = nrm(ks[18], (DEPTH, SSM_GROUPS, SSM_STATE, SSM_GROUP), (2 * SSM_GROUP) ** -0.5)
    ssm_b_im = nrm(ks[19], (DEPTH, SSM_GROUPS, SSM_STATE, SSM_GROUP), (2 * SSM_GROUP) ** -0.5)
    ssm_c_re = nrm(ks[20], (DEPTH, SSM_GROUPS, SSM_GROUP, SSM_STATE), (2 * SSM_STATE) ** -0.5)
    ssm_c_im = nrm(ks[21], (DEPTH, SSM_GROUPS, SSM_GROUP, SSM_STATE), (2 * SSM_STATE) ** -0.5)
    ssm_d = nrm(ks[22], (DEPTH, SSM_GROUPS, SSM_GROUP), 1.0)
    ssm_w_glu = nrm(ks[23], (DEPTH, SSM_WIDTH, 2 * SSM_WIDTH), SSM_WIDTH ** -0.5)
    ssm_b_glu = nrm(ks[24], (DEPTH, 2 * SSM_WIDTH), 0.01)
    mem_norm_w = 1.0 + nrm(ks[25], (DEPTH, D_MODEL), 0.02)
    w_kv = nrm(ks[26], (DEPTH, D_MODEL, 2 * BRANCH_WIDTH), D_MODEL ** -0.5)
    w_gate = nrm(ks[27], (DEPTH, N_BRANCH, GATE_RANK, D_MODEL), GATE_RANK ** -0.5)
    b_gate = nrm(ks[28], (DEPTH, N_BRANCH, D_MODEL), 0.01)
    w_branch = nrm(ks[29], (DEPTH, N_BRANCH, BRANCH_WIDTH, D_MODEL), BRANCH_WIDTH ** -0.5)
    w_out = nrm(ks[30], (DEPTH, D_MODEL, D_MODEL), D_MODEL ** -0.5)
    final_norm_w = 1.0 + nrm(ks[31], (D_MODEL,), 0.02)
    return {
        "x": x, "mem": mem, "norm_w": norm_w, "w_in": w_in,
        "dn_conv_w": dn_conv_w, "dn_a_log": dn_a_log, "dn_dt_bias": dn_dt_bias, "dn_norm_w": dn_norm_w,
        "lru_conv_w": lru_conv_w, "lru_conv_b": lru_conv_b, "lru_w_r": lru_w_r, "lru_b_r": lru_b_r,
        "lru_w_i": lru_w_i, "lru_b_i": lru_b_i, "lru_lambda": lru_lambda,
        "ssm_log_dt": ssm_log_dt, "ssm_a_re": ssm_a_re, "ssm_a_im": ssm_a_im,
        "ssm_b_re": ssm_b_re, "ssm_b_im": ssm_b_im, "ssm_c_re": ssm_c_re, "ssm_c_im": ssm_c_im,
        "ssm_d": ssm_d, "ssm_w_glu": ssm_w_glu, "ssm_b_glu": ssm_b_glu,
        "mem_norm_w": mem_norm_w, "w_kv": w_kv, "w_gate": w_gate, "b_gate": b_gate,
        "w_branch": w_branch, "w_out": w_out, "final_norm_w": final_norm_w,
    }


def reference(x, mem, norm_w, w_in, dn_conv_w, dn_a_log, dn_dt_bias, dn_norm_w,
              lru_conv_w, lru_conv_b, lru_w_r, lru_b_r, lru_w_i, lru_b_i, lru_lambda,
              ssm_log_dt, ssm_a_re, ssm_a_im, ssm_b_re, ssm_b_im, ssm_c_re, ssm_c_im,
              ssm_d, ssm_w_glu, ssm_b_glu, mem_norm_w, w_kv, w_gate, b_gate,
              w_branch, w_out, final_norm_w):
    offsets = np.cumsum(IN_SIZES)[:-1].tolist()
    for l in range(DEPTH):
        h = _rms_norm(x, norm_w[l])
        (dq, dk, dv, dz, d_beta, d_alpha, lx, lz, su, sz, mq, mz, g_low) = jnp.split(
            h @ w_in[l], offsets, axis=-1)
        o_a = _deltanet_branch(dq, dk, dv, dz, d_beta, d_alpha, dn_conv_w[l], dn_a_log[l],
                               dn_dt_bias[l], dn_norm_w[l])
        o_b = _rglru_branch(lx, lz, lru_conv_w[l], lru_conv_b[l], lru_w_r[l], lru_b_r[l],
                            lru_w_i[l], lru_b_i[l], lru_lambda[l])
        o_c = _s5_branch(su, sz, ssm_log_dt[l], ssm_a_re[l], ssm_a_im[l], ssm_b_re[l], ssm_b_im[l],
                         ssm_c_re[l], ssm_c_im[l], ssm_d[l], ssm_w_glu[l], ssm_b_glu[l])
        o_d = _memory_branch(mq, mz, mem, mem_norm_w[l], w_kv[l])
        branches = (o_a, o_b, o_c, o_d)
        merged = jnp.zeros(h.shape, jnp.float32)
        for n in range(N_BRANCH):
            gate = jax.nn.sigmoid((g_low @ w_gate[l, n] + b_gate[l, n]).astype(jnp.float32))
            merged = merged + gate * (branches[n] @ w_branch[l, n].astype(jnp.float32))
        x = x + (merged.astype(x.dtype) @ w_out[l]).astype(x.dtype)
    return _rms_norm(x, final_norm_w)
```

```python
import functools
import math

import jax
import jax.numpy as jnp
from jax import lax
from jax.experimental import pallas as pl
from jax.experimental.pallas import tpu as pltpu

F32 = jnp.float32
BF16 = jnp.bfloat16
HIGHEST = lax.Precision.HIGHEST

NORM_EPS = 1e-6
CONV_WIDTH = 4
HALO_ROWS = 8
N_BRANCH = 4
BRANCH_WIDTH = 1024
DN_HEAD_DIM = 128
DN_HEADS = 8
DN_CHUNK = 64
DN_TILE = 128
LRU_BLOCKS = 8
LRU_BLOCK = 128
LRU_C = 8.0
SSM_GROUP = 16
SSM_GROUPS = 64
SSM_STATE = 64
SSM_LANE_BLOCKS = 8
SSM_BLOCK_STATES = 512
MEM_HEADS = 4
MEM_HEAD_DIM = 256
GATE_RANK = 256
SMALL_COLS = 128

VMEM_LIMIT = 56 * 1024 * 1024


def _params(sem, vmem=None):
    return pltpu.CompilerParams(dimension_semantics=sem, vmem_limit_bytes=vmem)


def _sigmoid(x):
    return 1.0 / (1.0 + jnp.exp(-x))


def _silu(x):
    return x * _sigmoid(x)


def _softplus(x):
    return jnp.maximum(x, 0.0) + jnp.log1p(jnp.exp(-jnp.abs(x)))


def _dot(a, b):
    return jnp.dot(a.astype(BF16), b.astype(BF16), preferred_element_type=F32)


def _dot_nt(a, b):
    return lax.dot_general(a.astype(BF16), b.astype(BF16), (((1,), (1,)), ((), ())),
                           preferred_element_type=F32)


def _dot_f32(a, b):
    return jnp.dot(a, b, precision=HIGHEST, preferred_element_type=F32)


def _rmsnorm_kernel(x_ref, w_ref, o_ref):
    x = x_ref[...]
    var = jnp.mean(x * x, axis=-1, keepdims=True)
    o_ref[...] = (x * lax.rsqrt(var + NORM_EPS) * w_ref[...]).astype(o_ref.dtype)


def _rmsnorm(x, w, out_dtype, tm=256):
    m, d = x.shape
    tm = min(tm, m)
    return pl.pallas_call(
        _rmsnorm_kernel,
        out_shape=jax.ShapeDtypeStruct((m, d), out_dtype),
        grid=(m // tm,),
        in_specs=[pl.BlockSpec((tm, d), lambda i: (i, 0)),
                  pl.BlockSpec((1, d), lambda i: (0, 0))],
        out_specs=pl.BlockSpec((tm, d), lambda i: (i, 0)),
        compiler_params=_params(("parallel",)),
        name="rmsnorm",
    )(x, w.reshape(1, d))


def _mm_kernel(a_ref, w_ref, o_ref):
    o_ref[...] = jnp.dot(a_ref[...], w_ref[...], preferred_element_type=F32).astype(o_ref.dtype)


def _mm_res_kernel(a_ref, w_ref, r_ref, o_ref):
    acc = jnp.dot(a_ref[...], w_ref[...], preferred_element_type=F32)
    o_ref[...] = (r_ref[...] + acc).astype(o_ref.dtype)


def _matmul(a, w, out_dtype, residual=None, tm=1024, tn=512):
    m, k = a.shape
    n = w.shape[1]
    tm = min(tm, m)
    tn = min(tn, n)
    in_specs = [pl.BlockSpec((tm, k), lambda i, j: (i, 0)),
                pl.BlockSpec((k, tn), lambda i, j: (0, j))]
    args = [a, w]
    body = _mm_kernel
    if residual is not None:
        in_specs.append(pl.BlockSpec((tm, tn), lambda i, j: (i, j)))
        args.append(residual)
        body = _mm_res_kernel
    return pl.pallas_call(
        body,
        out_shape=jax.ShapeDtypeStruct((m, n), out_dtype),
        grid=(m // tm, n // tn),
        in_specs=in_specs,
        out_specs=pl.BlockSpec((tm, tn), lambda i, j: (i, j)),
        compiler_params=_params(("parallel", "arbitrary"), VMEM_LIMIT),
        name="matmul",
    )(*args)


def _fill_conv_buffer(buf, halo_ref, x_ref, t, tm):
    @pl.when(t == 0)
    def _():
        buf[0:HALO_ROWS, :] = jnp.zeros((HALO_ROWS, buf.shape[1]), F32)

    @pl.when(t != 0)
    def _():
        buf[0:HALO_ROWS, :] = halo_ref[...]

    buf[HALO_ROWS:HALO_ROWS + tm, :] = x_ref[...]


def _causal_conv(buf, cw_ref, tm):
    first = HALO_ROWS - (CONV_WIDTH - 1)
    acc = cw_ref[0:1, :] * buf[first:first + tm, :]
    for j in range(1, CONV_WIDTH):
        acc = acc + cw_ref[j:j + 1, :] * buf[first + j:first + j + tm, :]
    return acc


def _halo_index(seq_blocks, tile_blocks):
    def index(b, t, col):
        return (jnp.maximum(b * seq_blocks + t * tile_blocks - 1, 0), col)
    return index


def _dn_kernel(qkv_ref, halo_ref, z_ref, sm_ref, cw_ref, alog_ref, dtb_ref, nw_ref, o_ref,
               buf, state):
    t = pl.program_id(1)
    tm = DN_TILE
    d = DN_HEAD_DIM

    @pl.when(t == 0)
    def _():
        state[...] = jnp.zeros(state.shape, F32)

    _fill_conv_buffer(buf, halo_ref, qkv_ref, t, tm)
    qkv = _silu(_causal_conv(buf, cw_ref, tm))

    row = lax.broadcasted_iota(jnp.int32, (tm, tm), 0)
    col = lax.broadcasted_iota(jnp.int32, (tm, tm), 1)
    same_chunk = (row // DN_CHUNK) == (col // DN_CHUNK)
    causal = same_chunk & (row >= col)
    strict = same_chunk & (row > col)
    eye = (row == col).astype(F32)

    small = sm_ref[...]
    beta_all = _sigmoid(small)
    g_all = -jnp.exp(alog_ref[...]) * _softplus(small + dtb_ref[...])
    gc_all = _dot_f32(causal.astype(F32), g_all)
    gl_all = _dot_f32(same_chunk.astype(F32), g_all)
    gc_rows = gc_all.T

    z = z_ref[...]
    nw = nw_ref[...]
    for h in range(DN_HEADS):
        q = qkv[:, h * d:(h + 1) * d]
        k = qkv[:, BRANCH_WIDTH + h * d:BRANCH_WIDTH + (h + 1) * d]
        v = qkv[:, 2 * BRANCH_WIDTH + h * d:2 * BRANCH_WIDTH + (h + 1) * d]
        q = q * lax.rsqrt(jnp.sum(q * q, axis=-1, keepdims=True) + NORM_EPS) * (d ** -0.5)
        k = k * lax.rsqrt(jnp.sum(k * k, axis=-1, keepdims=True) + NORM_EPS)
        beta = beta_all[:, h:h + 1]
        gc = gc_all[:, DN_HEADS + h:DN_HEADS + h + 1]
        gl = gl_all[:, DN_HEADS + h:DN_HEADS + h + 1]
        gr = gc_rows[DN_HEADS + h:DN_HEADS + h + 1, :]
        decay = jnp.where(causal, jnp.exp(jnp.where(causal, gc - gr, 0.0)), 0.0)
        kb = k * beta
        vb = v * beta
        x = jnp.where(strict, -(_dot_nt(kb, k) * decay), 0.0)
        tinv = eye + x
        for _ in range(5):
            x = _dot_f32(x, x)
            tinv = tinv + _dot_f32(tinv, x)
        rhs = jnp.concatenate([vb, kb * jnp.exp(gc)], axis=-1)
        sol = _dot_f32(tinv, rhs)
        u = sol[:, :d]
        w = sol[:, d:]
        qk = jnp.where(causal, _dot_nt(q, k) * decay, 0.0)
        k_dec_t = (k * jnp.exp(gl - gc)).T
        q_dec = q * jnp.exp(gc)

        s = state[h]
        v_new = []
        inter = []
        for c in range(tm // DN_CHUNK):
            rows = slice(c * DN_CHUNK, (c + 1) * DN_CHUNK)
            vn = u[rows] - _dot(w[rows], s)
            inter.append(_dot(q_dec[rows], s))
            v_new.append(vn)
            padded = [vn if i == c else jnp.zeros_like(vn) for i in range(tm // DN_CHUNK)]
            s = s * jnp.exp(gl[c * DN_CHUNK:c * DN_CHUNK + 1, :]) + _dot(k_dec_t, jnp.concatenate(padded, axis=0))
        state[h] = s
        out = jnp.concatenate(inter, axis=0) + _dot(qk, jnp.concatenate(v_new, axis=0))
        var = jnp.mean(out * out, axis=-1, keepdims=True)
        out = out * lax.rsqrt(var + NORM_EPS) * nw
        o_ref[:, h * d:(h + 1) * d] = (out * _silu(z[:, h * d:(h + 1) * d])).astype(o_ref.dtype)


def _deltanet(proj, conv_w, a_log, dt_bias, norm_w, batch, seq, col):
    tm = DN_TILE
    width = 3 * BRANCH_WIDTH
    pad = (0, SMALL_COLS - 2 * DN_HEADS)
    alog = jnp.pad(jnp.concatenate([jnp.zeros((DN_HEADS,), F32), a_log]), pad).reshape(1, SMALL_COLS)
    dtb = jnp.pad(jnp.concatenate([jnp.zeros((DN_HEADS,), F32), dt_bias]), pad).reshape(1, SMALL_COLS)
    nt = seq // tm
    return pl.pallas_call(
        _dn_kernel,
        out_shape=jax.ShapeDtypeStruct((batch * seq, BRANCH_WIDTH), BF16),
        grid=(batch, nt),
        in_specs=[
            pl.BlockSpec((tm, width), lambda b, t: (b * nt + t, col["qkv"] // width)),
            pl.BlockSpec((HALO_ROWS, width),
                         functools.partial(_halo_index(seq // HALO_ROWS, tm // HALO_ROWS),
                                           col=col["qkv"] // width)),
            pl.BlockSpec((tm, BRANCH_WIDTH), lambda b, t: (b * nt + t, col["dz"] // BRANCH_WIDTH)),
            pl.BlockSpec((tm, SMALL_COLS), lambda b, t: (b * nt + t, col["small"] // SMALL_COLS)),
            pl.BlockSpec((CONV_WIDTH, width), lambda b, t: (0, 0)),
            pl.BlockSpec((1, SMALL_COLS), lambda b, t: (0, 0)),
            pl.BlockSpec((1, SMALL_COLS), lambda b, t: (0, 0)),
            pl.BlockSpec((1, DN_HEAD_DIM), lambda b, t: (0, 0)),
        ],
        out_specs=pl.BlockSpec((tm, BRANCH_WIDTH), lambda b, t: (b * nt + t, 0)),
        scratch_shapes=[pltpu.VMEM((HALO_ROWS + tm, width), F32),
                        pltpu.VMEM((DN_HEADS, DN_HEAD_DIM, DN_HEAD_DIM), F32)],
        compiler_params=_params(("parallel", "arbitrary"), VMEM_LIMIT),
        name="deltanet",
    )(proj, proj, proj, proj, conv_w, alog, dtb, norm_w.reshape(1, DN_HEAD_DIM))


def _lru_kernel(x_ref, halo_ref, z_ref, cw_ref, cb_ref, w_ref, b_ref, lam_ref, o_ref,
                buf, a_s, b_s, carry):
    t = pl.program_id(1)
    tm = x_ref.shape[0]
    width = x_ref.shape[1]

    @pl.when(t == 0)
    def _():
        carry[...] = jnp.zeros(carry.shape, F32)

    _fill_conv_buffer(buf, halo_ref, x_ref, t, tm)
    xc = _causal_conv(buf, cw_ref, tm) + cb_ref[...]
    gates = []
    for n in range(LRU_BLOCKS):
        gates.append(_dot(xc[:, n * LRU_BLOCK:(n + 1) * LRU_BLOCK], w_ref[n]))
    r = jnp.concatenate([g[:, :LRU_BLOCK] for g in gates], axis=-1)
    i = jnp.concatenate([g[:, LRU_BLOCK:] for g in gates], axis=-1)
    r = _sigmoid(r + b_ref[0:1, :])
    i = _sigmoid(i + b_ref[1:2, :])
    log_a = -LRU_C * r * _softplus(-lam_ref[...])
    a_s[...] = jnp.exp(log_a)
    b_s[...] = jnp.sqrt(1.0 - jnp.exp(2.0 * log_a)) * (i * xc)

    sub = lax.broadcasted_iota(jnp.int32, (HALO_ROWS, width), 0)

    def group(g, h_prev):
        rows = pl.ds(pl.multiple_of(g * HALO_ROWS, HALO_ROWS), HALO_ROWS)
        a = a_s[rows, :]
        b = b_s[rows, :]
        for shift in (1, 2, 4):
            keep = sub >= shift
            a_prev = jnp.where(keep, pltpu.roll(a, shift, 0), 1.0)
            b_prev = jnp.where(keep, pltpu.roll(b, shift, 0), 0.0)
            b = a * b_prev + b
            a = a * a_prev
        h = a * h_prev + b
        b_s[rows, :] = h
        return jnp.broadcast_to(h[HALO_ROWS - 1:HALO_ROWS, :], (HALO_ROWS, width))

    carry[...] = lax.fori_loop(0, tm // HALO_ROWS, group, carry[...])
    o_ref[...] = (b_s[...] * _silu(z_ref[...])).astype(o_ref.dtype)


def _rglru(proj, conv_w, conv_b, w_r, b_r, w_i, b_i, lam, batch, seq, col, tm=256):
    tm = min(tm, seq)
    width = BRANCH_WIDTH
    nt = seq // tm
    w_ri = jnp.concatenate([w_r, w_i], axis=-1).astype(BF16)
    b_ri = jnp.stack([b_r, b_i])
    return pl.pallas_call(
        _lru_kernel,
        out_shape=jax.ShapeDtypeStruct((batch * seq, width), BF16),
        grid=(batch, nt),
        in_specs=[
            pl.BlockSpec((tm, width), lambda b, t: (b * nt + t, col["lx"] // width)),
            pl.BlockSpec((HALO_ROWS, width),
                         functools.partial(_halo_index(seq // HALO_ROWS, tm // HALO_ROWS),
                                           col=col["lx"] // width)),
            pl.BlockSpec((tm, width), lambda b, t: (b * nt + t, col["lz"] // width)),
            pl.BlockSpec((CONV_WIDTH, width), lambda b, t: (0, 0)),
            pl.BlockSpec((1, width), lambda b, t: (0, 0)),
            pl.BlockSpec((LRU_BLOCKS, LRU_BLOCK, 2 * LRU_BLOCK), lambda b, t: (0, 0, 0)),
            pl.BlockSpec((2, width), lambda b, t: (0, 0)),
            pl.BlockSpec((1, width), lambda b, t: (0, 0)),
        ],
        out_specs=pl.BlockSpec((tm, width), lambda b, t: (b * nt + t, 0)),
        scratch_shapes=[pltpu.VMEM((HALO_ROWS + tm, width), F32),
                        pltpu.VMEM((tm, width), F32),
                        pltpu.VMEM((tm, width), F32),
                        pltpu.VMEM((HALO_ROWS, width), F32)],
        compiler_params=_params(("parallel", "arbitrary"), VMEM_LIMIT),
        name="rglru",
    )(proj, proj, proj, conv_w, conv_b.reshape(1, width), w_ri, b_ri, lam.reshape(1, width))


def _s5_discretize_kernel(ldt_ref, are_ref, aim_ref, bre_ref, bim_ref,
                          bbre_ref, bbim_ref, pre_ref, pim_ref):
    dt = jnp.exp(ldt_ref[...])
    a_re = are_ref[...]
    a_im = aim_ref[...]
    mag = jnp.exp(dt * a_re)
    ab_re = mag * jnp.cos(dt * a_im)
    ab_im = mag * jnp.sin(dt * a_im)
    den = a_re * a_re + a_im * a_im
    f_re = ((ab_re - 1.0) * a_re + ab_im * a_im) / den
    f_im = (ab_im * a_re - (ab_re - 1.0) * a_im) / den
    b_re = bre_ref[...]
    b_im = bim_ref[...]
    bbre_ref[...] = f_re * b_re - f_im * b_im
    bbim_ref[...] = f_re * b_im + f_im * b_re
    p_re, p_im = ab_re, ab_im
    pre_ref[0] = p_re
    pim_ref[0] = p_im
    for p in range(1, HALO_ROWS):
        p_re, p_im = p_re * ab_re - p_im * ab_im, p_re * ab_im + p_im * ab_re
        pre_ref[p] = p_re
        pim_ref[p] = p_im


def _s5_discretize(log_dt, a_re, a_im, b_re, b_im):
    rows = SSM_GROUPS * SSM_GROUP
    rep = lambda p: jnp.repeat(p, SSM_GROUP, axis=0)
    ldt = rep(jnp.broadcast_to(log_dt[:, None], (SSM_GROUPS, SSM_STATE)))
    to_rows = lambda b: b.transpose(0, 2, 1).reshape(rows, SSM_STATE)
    shp = jax.ShapeDtypeStruct((rows, SSM_STATE), F32)
    pshp = jax.ShapeDtypeStruct((HALO_ROWS, rows, SSM_STATE), F32)
    bb_re, bb_im, p_re, p_im = pl.pallas_call(
        _s5_discretize_kernel, out_shape=(shp, shp, pshp, pshp), name="s5_discretize",
    )(ldt, rep(a_re), rep(a_im), to_rows(b_re), to_rows(b_im))
    return bb_re, bb_im, p_re[:, ::SSM_GROUP, :], p_im[:, ::SSM_GROUP, :]


def _s5_kernel(u_ref, z_ref, bw_ref, cw_ref, d_ref, s1_ref, s2_ref, s4_ref, pc_ref,
               wg_ref, bg_ref, o_ref, xs, y_s, carry):
    t = pl.program_id(1)
    tm = u_ref.shape[0]
    ns = SSM_BLOCK_STATES
    ch = SSM_GROUP * (SSM_GROUPS // SSM_LANE_BLOCKS)

    @pl.when(t == 0)
    def _():
        carry[...] = jnp.zeros(carry.shape, F32)

    for cb in range(SSM_LANE_BLOCKS):
        ub = u_ref[:, cb * ch:(cb + 1) * ch]
        xs[...] = _dot(ub, bw_ref[cb])
        steps = ((1, s1_ref), (2, s2_ref), (4, s4_ref))

        def group(g, c, cb=cb, steps=steps):
            rows = pl.ds(pl.multiple_of(g * HALO_ROWS, HALO_ROWS), HALO_ROWS)
            xr = xs[rows, 0:ns]
            xi = xs[rows, ns:2 * ns]
            for shift, s_ref in steps:
                mr = s_ref[cb, :, 0:ns]
                mi = s_ref[cb, :, ns:2 * ns]
                pr = pltpu.roll(xr, shift, 0)
                pi = pltpu.roll(xi, shift, 0)
                xr, xi = xr + (mr * pr - mi * pi), xi + (mr * pi + mi * pr)
            cr, ci = c
            mr = pc_ref[cb, :, 0:ns]
            mi = pc_ref[cb, :, ns:2 * ns]
            xr, xi = xr + (mr * cr - mi * ci), xi + (mr * ci + mi * cr)
            xs[rows, 0:ns] = xr
            xs[rows, ns:2 * ns] = xi
            last = slice(HALO_ROWS - 1, HALO_ROWS)
            return (jnp.broadcast_to(xr[last, :], (HALO_ROWS, ns)),
                    jnp.broadcast_to(xi[last, :], (HALO_ROWS, ns)))

        c0 = (carry[cb, :, 0:ns], carry[cb, :, ns:2 * ns])
        cr, ci = lax.fori_loop(0, tm // HALO_ROWS, group, c0)
        carry[cb, :, 0:ns] = cr
        carry[cb, :, ns:2 * ns] = ci
        y_s[:, cb * ch:(cb + 1) * ch] = _dot(xs[...], cw_ref[cb]) + d_ref[:, cb * ch:(cb + 1) * ch] * ub

    y = y_s[...]
    y = 0.5 * y * (1.0 + jnp.tanh(math.sqrt(2.0 / math.pi) * (y + 0.044715 * (y * y * y))))
    glu = _dot(y, wg_ref[...]) + bg_ref[...]
    width = y.shape[1]
    o_ref[...] = (glu[:, :width] * _sigmoid(glu[:, width:]) * _silu(z_ref[...])).astype(o_ref.dtype)


def _s5(proj, log_dt, a_re, a_im, b_re, b_im, c_re, c_im, d_skip, w_glu, b_glu, batch, seq, col, tm=256):
    tm = min(tm, seq)
    width = BRANCH_WIDTH
    nt = seq // tm
    nb = SSM_LANE_BLOCKS
    gb = SSM_GROUPS // nb
    ns = SSM_BLOCK_STATES
    bb_re, bb_im, p_re, p_im = _s5_discretize(log_dt, a_re, a_im, b_re, b_im)
    eye = jnp.eye(gb, dtype=F32)

    def in_block(bb):
        bb = bb.reshape(nb, gb, SSM_GROUP, SSM_STATE)
        return jnp.einsum("bgcn,gh->bgchn", bb, eye).reshape(nb, gb * SSM_GROUP, ns)

    def out_block(c):
        c = c.reshape(nb, gb, SSM_GROUP, SSM_STATE)
        return jnp.einsum("bgcn,gh->bgnhc", c, eye).reshape(nb, ns, gb * SSM_GROUP)

    bw = jnp.concatenate([in_block(bb_re), in_block(bb_im)], axis=-1).astype(BF16)
    cw = jnp.concatenate([out_block(c_re), -out_block(c_im)], axis=1).astype(BF16)

    def lanes(p):
        return p.reshape(p.shape[0], nb, ns).transpose(1, 0, 2)

    sub = jnp.arange(HALO_ROWS)[:, None, None]

    def step_mult(shift):
        pr = jnp.where(sub >= shift, p_re[shift - 1][None], 0.0)
        pi = jnp.where(sub >= shift, p_im[shift - 1][None], 0.0)
        return jnp.concatenate([lanes(pr), lanes(pi)], axis=-1)

    pc = jnp.concatenate([lanes(p_re), lanes(p_im)], axis=-1)
    mult_spec = pl.BlockSpec((nb, HALO_ROWS, 2 * ns), lambda b, t: (0, 0, 0))
    return pl.pallas_call(
        _s5_kernel,
        out_shape=jax.ShapeDtypeStruct((batch * seq, width), BF16),
        grid=(batch, nt),
        in_specs=[
            pl.BlockSpec((tm, width), lambda b, t: (b * nt + t, col["su"] // width)),
            pl.BlockSpec((tm, width), lambda b, t: (b * nt + t, col["sz"] // width)),
            pl.BlockSpec((nb, gb * SSM_GROUP, 2 * ns), lambda b, t: (0, 0, 0)),
            pl.BlockSpec((nb, 2 * ns, gb * SSM_GROUP), lambda b, t: (0, 0, 0)),
            pl.BlockSpec((1, width), lambda b, t: (0, 0)),
            mult_spec, mult_spec, mult_spec, mult_spec,
            pl.BlockSpec((width, 2 * width), lambda b, t: (0, 0)),
            pl.BlockSpec((1, 2 * width), lambda b, t: (0, 0)),
        ],
        out_specs=pl.BlockSpec((tm, width), lambda b, t: (b * nt + t, 0)),
        scratch_shapes=[pltpu.VMEM((tm, 2 * ns), F32),
                        pltpu.VMEM((tm, width), F32),
                        pltpu.VMEM((nb, HALO_ROWS, 2 * ns), F32)],
        compiler_params=_params(("parallel", "arbitrary"), VMEM_LIMIT),
        name="s5",
    )(proj, proj, bw, cw, d_skip.reshape(1, width), step_mult(1), step_mult(2), step_mult(4), pc,
      w_glu.astype(BF16), b_glu.reshape(1, 2 * width))


def _mem_attn_kernel(q_ref, z_ref, k_ref, v_ref, o_ref):
    dh = MEM_HEAD_DIM
    z = z_ref[...]
    for h in range(MEM_HEADS):
        cols = slice(h * dh, (h + 1) * dh)
        s = _dot_nt(q_ref[:, cols], k_ref[:, cols]) * (dh ** -0.5)
        s = s - jnp.max(s, axis=-1, keepdims=True)
        p = jnp.exp(s)
        p = p / jnp.sum(p, axis=-1, keepdims=True)
        o = _dot(p, v_ref[:, cols])
        o_ref[:, cols] = (o * _silu(z[:, cols])).astype(o_ref.dtype)


def _mem_attn(proj, kv, batch, seq, mem_len, col, tm=256):
    tm = min(tm, seq)
    width = BRANCH_WIDTH
    nt = seq // tm
    return pl.pallas_call(
        _mem_attn_kernel,
        out_shape=jax.ShapeDtypeStruct((batch * seq, width), BF16),
        grid=(batch, nt),
        in_specs=[
            pl.BlockSpec((tm, width), lambda b, t: (b * nt + t, col["mq"] // width)),
            pl.BlockSpec((tm, width), lambda b, t: (b * nt + t, col["mz"] // width)),
            pl.BlockSpec((mem_len, width), lambda b, t: (b, 0)),
            pl.BlockSpec((mem_len, width), lambda b, t: (b, 1)),
        ],
        out_specs=pl.BlockSpec((tm, width), lambda b, t: (b * nt + t, 0)),
        compiler_params=_params(("parallel", "parallel"), VMEM_LIMIT),
        name="mem_attn",
    )(proj, proj, kv, kv)


def _merge_kernel(g_ref, oa_ref, ob_ref, oc_ref, od_ref, wg_ref, bg_ref, wb_ref, o_ref):
    g_low = g_ref[...]
    acc = None
    for n, br_ref in enumerate((oa_ref, ob_ref, oc_ref, od_ref)):
        gate = _sigmoid(_dot(g_low, wg_ref[n]) + bg_ref[n:n + 1, :])
        term = gate * jnp.dot(br_ref[...], wb_ref[n], preferred_element_type=F32)
        acc = term if acc is None else acc + term
    o_ref[...] = acc.astype(o_ref.dtype)


def _merge(proj, branches, w_gate, b_gate, w_branch, col, tm=1024, tn=512):
    m = proj.shape[0]
    d_model = w_gate.shape[-1]
    tm = min(tm, m)
    br_spec = pl.BlockSpec((tm, BRANCH_WIDTH), lambda i, j: (i, 0))
    return pl.pallas_call(
        _merge_kernel,
        out_shape=jax.ShapeDtypeStruct((m, d_model), BF16),
        grid=(m // tm, d_model // tn),
        in_specs=[
            pl.BlockSpec((tm, GATE_RANK), lambda i, j: (i, col["glow"] // GATE_RANK)),
            br_spec, br_spec, br_spec, br_spec,
            pl.BlockSpec((N_BRANCH, GATE_RANK, tn), lambda i, j: (0, 0, j)),
            pl.BlockSpec((N_BRANCH, tn), lambda i, j: (0, j)),
            pl.BlockSpec((N_BRANCH, BRANCH_WIDTH, tn), lambda i, j: (0, 0, j)),
        ],
        out_specs=pl.BlockSpec((tm, tn), lambda i, j: (i, j)),
        compiler_params=_params(("parallel", "arbitrary"), VMEM_LIMIT),
        name="merge",
    )(proj, *branches, w_gate.astype(BF16), b_gate, w_branch.astype(BF16))


def _in_proj_layout(w_in_l):
    d_model = w_in_l.shape[0]
    a_end = 4 * BRANCH_WIDTH
    small = 2 * DN_HEADS
    main = 6 * BRANCH_WIDTH + GATE_RANK
    pad = (-(a_end + main + small)) % 512
    w = jnp.concatenate([w_in_l[:, :a_end], w_in_l[:, a_end + small:], w_in_l[:, a_end:a_end + small],
                         jnp.zeros((d_model, pad), w_in_l.dtype)], axis=1).astype(BF16)
    w1 = BRANCH_WIDTH
    col = {"qkv": 0, "dz": 3 * w1, "lx": 4 * w1, "lz": 5 * w1, "su": 6 * w1, "sz": 7 * w1,
           "mq": 8 * w1, "mz": 9 * w1, "glow": 10 * w1, "small": 10 * w1 + GATE_RANK}
    return w, col


def kernel(x, mem, norm_w, w_in, dn_conv_w, dn_a_log, dn_dt_bias, dn_norm_w, lru_conv_w, lru_conv_b, lru_w_r, lru_b_r, lru_w_i, lru_b_i, lru_lambda, ssm_log_dt, ssm_a_re, ssm_a_im, ssm_b_re, ssm_b_im, ssm_c_re, ssm_c_im, ssm_d, ssm_w_glu, ssm_b_glu, mem_norm_w, w_kv, w_gate, b_gate, w_branch, w_out, final_norm_w):
    batch, seq, d_model = x.shape
    mem_len = mem.shape[1]
    depth = w_in.shape[0]
    xf = x.reshape(batch * seq, d_model)
    memf = mem.reshape(batch * mem_len, d_model)
    for l in range(depth):
        w_cat, col = _in_proj_layout(w_in[l])
        h = _rmsnorm(xf, norm_w[l], BF16)
        proj = _matmul(h, w_cat, F32)
        o_a = _deltanet(proj, dn_conv_w[l], dn_a_log[l], dn_dt_bias[l], dn_norm_w[l], batch, seq, col)
        o_b = _rglru(proj, lru_conv_w[l], lru_conv_b[l], lru_w_r[l], lru_b_r[l], lru_w_i[l], lru_b_i[l],
                     lru_lambda[l], batch, seq, col)
        o_c = _s5(proj, ssm_log_dt[l], ssm_a_re[l], ssm_a_im[l], ssm_b_re[l], ssm_b_im[l],
                  ssm_c_re[l], ssm_c_im[l], ssm_d[l].reshape(-1), ssm_w_glu[l], ssm_b_glu[l], batch, seq, col)
        m_n = _rmsnorm(memf, mem_norm_w[l], BF16)
        kv = _matmul(m_n, w_kv[l].astype(BF16), BF16)
        o_d = _mem_attn(proj, kv, batch, seq, mem_len, col)
        merged = _merge(proj, (o_a, o_b, o_c, o_d), w_gate[l], b_gate[l], w_branch[l], col)
        xf = _matmul(merged, w_out[l].astype(BF16), F32, residual=xf)
    out = _rmsnorm(xf, final_norm_w, F32)
    return out.reshape(batch, seq, d_model)
```

```python
import functools
import math

import jax
import jax.numpy as jnp
from jax import lax
from jax.experimental import pallas as pl
from jax.experimental.pallas import tpu as pltpu

F32 = jnp.float32
BF16 = jnp.bfloat16
HIGHEST = lax.Precision.HIGHEST

NORM_EPS = 1e-6
CONV_WIDTH = 4
HALO_ROWS = 8
N_BRANCH = 4
BRANCH_WIDTH = 1024
DN_HEAD_DIM = 128
DN_HEADS = 8
DN_CHUNK = 64
DN_TILE = 128
LRU_BLOCKS = 8
LRU_BLOCK = 128
LRU_C = 8.0
SSM_GROUP = 16
SSM_GROUPS = 64
SSM_STATE = 64
SSM_LANE_BLOCKS = 8
SSM_BLOCK_STATES = 512
MEM_HEADS = 4
MEM_HEAD_DIM = 256
GATE_RANK = 256
SMALL_COLS = 128

VMEM_LIMIT = 56 * 1024 * 1024


def _params(sem, vmem=None):
    return pltpu.CompilerParams(dimension_semantics=sem, vmem_limit_bytes=vmem)


def _sigmoid(x):
    return 1.0 / (1.0 + jnp.exp(-x))


def _silu(x):
    return x * _sigmoid(x)


def _softplus(x):
    return jnp.maximum(x, 0.0) + jnp.log1p(jnp.exp(-jnp.abs(x)))


def _dot(a, b):
    return jnp.dot(a.astype(BF16), b.astype(BF16), preferred_element_type=F32)


def _dot_nt(a, b):
    return lax.dot_general(a.astype(BF16), b.astype(BF16), (((1,), (1,)), ((), ())),
                           preferred_element_type=F32)


def _dot_f32(a, b):
    return jnp.dot(a, b, precision=HIGHEST, preferred_element_type=F32)


def _rmsnorm_kernel(x_ref, w_ref, o_ref):
    x = x_ref[...]
    var = jnp.mean(x * x, axis=-1, keepdims=True)
    o_ref[...] = (x * lax.rsqrt(var + NORM_EPS) * w_ref[...]).astype(o_ref.dtype)


def _rmsnorm(x, w, out_dtype, tm=256):
    m, d = x.shape
    tm = min(tm, m)
    return pl.pallas_call(
        _rmsnorm_kernel,
        out_shape=jax.ShapeDtypeStruct((m, d), out_dtype),
        grid=(m // tm,),
        in_specs=[pl.BlockSpec((tm, d), lambda i: (i, 0)),
                  pl.BlockSpec((1, d), lambda i: (0, 0))],
        out_specs=pl.BlockSpec((tm, d), lambda i: (i, 0)),
        compiler_params=_params(("parallel",)),
        name="rmsnorm",
    )(x, w.reshape(1, d))


def _mm_kernel(a_ref, w_ref, o_ref):
    o_ref[...] = jnp.dot(a_ref[...], w_ref[...], preferred_element_type=F32).astype(o_ref.dtype)


def _mm_res_kernel(a_ref, w_ref, r_ref, o_ref):
    acc = jnp.dot(a_ref[...], w_ref[...], preferred_element_type=F32)
    o_ref[...] = (r_ref[...] + acc).astype(o_ref.dtype)


def _matmul(a, w, out_dtype, residual=None, tm=1024, tn=512):
    m, k = a.shape
    n = w.shape[1]
    tm = min(tm, m)
    tn = min(tn, n)
    in_specs = [pl.BlockSpec((tm, k), lambda i, j: (i, 0)),
                pl.BlockSpec((k, tn), lambda i, j: (0, j))]
    args = [a, w]
    body = _mm_kernel
    if residual is not None:
        in_specs.append(pl.BlockSpec((tm, tn), lambda i, j: (i, j)))
        args.append(residual)
        body = _mm_res_kernel
    return pl.pallas_call(
        body,
        out_shape=jax.ShapeDtypeStruct((m, n), out_dtype),
        grid=(m // tm, n // tn),
        in_specs=in_specs,
        out_specs=pl.BlockSpec((tm, tn), lambda i, j: (i, j)),
        compiler_params=_params(("parallel", "arbitrary"), VMEM_LIMIT),
        name="matmul",
    )(*args)


def _fill_conv_buffer(buf, halo_ref, x_ref, t, tm):
    @pl.when(t == 0)
    def _():
        buf[0:HALO_ROWS, :] = jnp.zeros((HALO_ROWS, buf.shape[1]), F32)

    @pl.when(t != 0)
    def _():
        buf[0:HALO_ROWS, :] = halo_ref[...]

    buf[HALO_ROWS:HALO_ROWS + tm, :] = x_ref[...]


def _causal_conv(buf, cw_ref, tm):
    first = HALO_ROWS - (CONV_WIDTH - 1)
    last = CONV_WIDTH - 1
    acc = cw_ref[last:last + 1, :] * buf[first + last:first + last + tm, :]
    for j in range(last - 1, -1, -1):
        acc = acc + cw_ref[j:j + 1, :] * buf[first + j:first + j + tm, :]
    return acc


def _halo_index(seq_blocks, tile_blocks):
    def index(b, t, col):
        return (jnp.maximum(b * seq_blocks + t * tile_blocks - 1, 0), col)
    return index


def _dn_kernel(qkv_ref, halo_ref, z_ref, sm_ref, cw_ref, alog_ref, dtb_ref, nw_ref, o_ref,
               buf, state):
    t = pl.program_id(1)
    tm = DN_TILE
    d = DN_HEAD_DIM

    @pl.when(t == 0)
    def _():
        state[...] = jnp.zeros(state.shape, F32)

    _fill_conv_buffer(buf, halo_ref, qkv_ref, t, tm)
    qkv = _silu(_causal_conv(buf, cw_ref, tm))

    row = lax.broadcasted_iota(jnp.int32, (tm, tm), 0)
    col = lax.broadcasted_iota(jnp.int32, (tm, tm), 1)
    same_chunk = (row // DN_CHUNK) == (col // DN_CHUNK)
    causal = same_chunk & (row >= col)
    strict = same_chunk & (row > col)

    small = sm_ref[...]
    beta_all = _sigmoid(small)
    g_all = -jnp.exp(alog_ref[...]) * _softplus(small + dtb_ref[...])
    gc_all = _dot_f32(causal.astype(F32), g_all)
    gl_all = _dot_f32(same_chunk.astype(F32), g_all)
    gc_rows = gc_all.T

    z = z_ref[...]
    nw = nw_ref[...]
    heads = range(DN_HEADS)
    chunks = range(tm // DN_CHUNK)
    zero_chunk = jnp.zeros((DN_CHUNK, 2 * d), F32)

    q, k, kb, qd, kd_t, decay, sol, gl = [], [], [], [], [], [], [], []
    for h in heads:
        qh = qkv[:, h * d:(h + 1) * d]
        kh = qkv[:, BRANCH_WIDTH + h * d:BRANCH_WIDTH + (h + 1) * d]
        vh = qkv[:, 2 * BRANCH_WIDTH + h * d:2 * BRANCH_WIDTH + (h + 1) * d]
        qh = qh * lax.rsqrt(jnp.sum(qh * qh, axis=-1, keepdims=True) + NORM_EPS) * (d ** -0.5)
        kh = kh * lax.rsqrt(jnp.sum(kh * kh, axis=-1, keepdims=True) + NORM_EPS)
        beta = beta_all[:, h:h + 1]
        gc = gc_all[:, DN_HEADS + h:DN_HEADS + h + 1]
        glh = gl_all[:, DN_HEADS + h:DN_HEADS + h + 1]
        gr = gc_rows[DN_HEADS + h:DN_HEADS + h + 1, :]
        decay.append(jnp.where(causal, jnp.exp(jnp.where(causal, gc - gr, 0.0)), 0.0))
        eg = jnp.exp(gc)
        kbh = kh * beta
        q.append(qh.astype(BF16))
        k.append(kh.astype(BF16))
        kb.append(kbh.astype(BF16))
        qd.append(qh * eg)
        kd_t.append((kh * jnp.exp(glh - gc)).T.astype(BF16))
        sol.append(jnp.concatenate([vh * beta, kbh * eg], axis=-1))
        gl.append(glh)

    x, qk = [], []
    for h in heads:
        both = lax.dot_general(jnp.concatenate([kb[h], q[h]], axis=0), k[h], (((1,), (1,)), ((), ())),
                               preferred_element_type=F32)
        x.append(jnp.where(strict, -(both[:tm] * decay[h]), 0.0))
        qk.append(jnp.where(causal, both[tm:] * decay[h], 0.0).astype(BF16))

    n_square = 5
    for it in range(n_square + 1):
        for h in heads:
            xb = x[h].astype(BF16)
            sb = sol[h].astype(BF16)
            if it < n_square:
                y = jnp.dot(xb, jnp.concatenate([xb, sb], axis=-1), preferred_element_type=F32)
                x[h] = y[:, :tm]
                sol[h] = sol[h] + y[:, tm:]
            else:
                sol[h] = sol[h] + jnp.dot(xb, sb, preferred_element_type=F32)

    kw, ku = [], []
    for h in heads:
        kw_h, ku_h = [], []
        for c in chunks:
            rows = slice(c * DN_CHUNK, (c + 1) * DN_CHUNK)
            wu = jnp.concatenate([sol[h][rows, d:], sol[h][rows, :d]], axis=-1)
            padded = jnp.concatenate([wu if i == c else zero_chunk for i in chunks], axis=0)
            prod = jnp.dot(kd_t[h], padded.astype(BF16), preferred_element_type=F32)
            kw_h.append(prod[:, :d].astype(BF16))
            ku_h.append(prod[:, d:])
        kw.append(kw_h)
        ku.append(ku_h)

    s = [state[h] for h in heads]
    inter = [[] for _ in heads]
    v_new = [[] for _ in heads]
    for c in chunks:
        rows = slice(c * DN_CHUNK, (c + 1) * DN_CHUNK)
        for h in heads:
            sb = s[h].astype(BF16)
            lhs = jnp.concatenate([qd[h][rows], sol[h][rows, d:]], axis=0).astype(BF16)
            r = jnp.dot(lhs, sb, preferred_element_type=F32)
            inter[h].append(r[:DN_CHUNK])
            v_new[h].append(sol[h][rows, :d] - r[DN_CHUNK:])
            s[h] = (s[h] * jnp.exp(gl[h][c * DN_CHUNK:c * DN_CHUNK + 1, :])
                    - jnp.dot(kw[h][c], sb, preferred_element_type=F32) + ku[h][c])
    for h in heads:
        state[h] = s[h]
        out = jnp.concatenate(inter[h], axis=0) + jnp.dot(
            qk[h], jnp.concatenate(v_new[h], axis=0).astype(BF16), preferred_element_type=F32)
        var = jnp.mean(out * out, axis=-1, keepdims=True)
        out = out * lax.rsqrt(var + NORM_EPS) * nw
        o_ref[:, h * d:(h + 1) * d] = (out * _silu(z[:, h * d:(h + 1) * d])).astype(o_ref.dtype)


def _deltanet(proj, conv_w, a_log, dt_bias, norm_w, batch, seq, col):
    tm = DN_TILE
    width = 3 * BRANCH_WIDTH
    pad = (0, SMALL_COLS - 2 * DN_HEADS)
    alog = jnp.pad(jnp.concatenate([jnp.zeros((DN_HEADS,), F32), a_log]), pad).reshape(1, SMALL_COLS)
    dtb = jnp.pad(jnp.concatenate([jnp.zeros((DN_HEADS,), F32), dt_bias]), pad).reshape(1, SMALL_COLS)
    nt = seq // tm
    return pl.pallas_call(
        _dn_kernel,
        out_shape=jax.ShapeDtypeStruct((batch * seq, BRANCH_WIDTH), BF16),
        grid=(batch, nt),
        in_specs=[
            pl.BlockSpec((tm, width), lambda b, t: (b * nt + t, col["qkv"] // width)),
            pl.BlockSpec((HALO_ROWS, width),
                         functools.partial(_halo_index(seq // HALO_ROWS, tm // HALO_ROWS),
                                           col=col["qkv"] // width)),
            pl.BlockSpec((tm, BRANCH_WIDTH), lambda b, t: (b * nt + t, col["dz"] // BRANCH_WIDTH)),
            pl.BlockSpec((tm, SMALL_COLS), lambda b, t: (b * nt + t, col["small"] // SMALL_COLS)),
            pl.BlockSpec((CONV_WIDTH, width), lambda b, t: (0, 0)),
            pl.BlockSpec((1, SMALL_COLS), lambda b, t: (0, 0)),
            pl.BlockSpec((1, SMALL_COLS), lambda b, t: (0, 0)),
            pl.BlockSpec((1, DN_HEAD_DIM), lambda b, t: (0, 0)),
        ],
        out_specs=pl.BlockSpec((tm, BRANCH_WIDTH), lambda b, t: (b * nt + t, 0)),
        scratch_shapes=[pltpu.VMEM((HALO_ROWS + tm, width), F32),
                        pltpu.VMEM((DN_HEADS, DN_HEAD_DIM, DN_HEAD_DIM), F32)],
        compiler_params=_params(("parallel", "arbitrary"), VMEM_LIMIT),
        name="deltanet",
    )(proj, proj, proj, proj, conv_w, alog, dtb, norm_w.reshape(1, DN_HEAD_DIM))


def _lru_kernel(x_ref, halo_ref, z_ref, cw_ref, cb_ref, w_ref, b_ref, lam_ref, o_ref,
                buf, a_s, b_s, carry):
    t = pl.program_id(1)
    tm = x_ref.shape[0]
    width = x_ref.shape[1]

    @pl.when(t == 0)
    def _():
        carry[...] = jnp.zeros(carry.shape, F32)

    _fill_conv_buffer(buf, halo_ref, x_ref, t, tm)
    xc = _causal_conv(buf, cw_ref, tm) + cb_ref[...]
    gates = []
    for n in range(LRU_BLOCKS):
        gates.append(_dot(xc[:, n * LRU_BLOCK:(n + 1) * LRU_BLOCK], w_ref[n]))
    r = jnp.concatenate([g[:, :LRU_BLOCK] for g in gates], axis=-1)
    i = jnp.concatenate([g[:, LRU_BLOCK:] for g in gates], axis=-1)
    r = _sigmoid(r + b_ref[0:1, :])
    i = _sigmoid(i + b_ref[1:2, :])
    log_a = -LRU_C * r * _softplus(-lam_ref[...])
    a_s[...] = jnp.exp(log_a)
    b_s[...] = jnp.sqrt(1.0 - jnp.exp(2.0 * log_a)) * (i * xc)

    sub = lax.broadcasted_iota(jnp.int32, (HALO_ROWS, width), 0)

    def group(g, h_prev):
        rows = pl.ds(pl.multiple_of(g * HALO_ROWS, HALO_ROWS), HALO_ROWS)
        a = a_s[rows, :]
        b = b_s[rows, :]
        for shift in (1, 2, 4):
            keep = sub >= shift
            a_prev = jnp.where(keep, pltpu.roll(a, shift, 0), 1.0)
            b_prev = jnp.where(keep, pltpu.roll(b, shift, 0), 0.0)
            b = a * b_prev + b
            a = a * a_prev
        h = a * h_prev + b
        b_s[rows, :] = h
        return jnp.broadcast_to(h[HALO_ROWS - 1:HALO_ROWS, :], (HALO_ROWS, width))

    carry[...] = lax.fori_loop(0, tm // HALO_ROWS, group, carry[...])
    o_ref[...] = (b_s[...] * _silu(z_ref[...])).astype(o_ref.dtype)


def _rglru(proj, conv_w, conv_b, w_r, b_r, w_i, b_i, lam, batch, seq, col, tm=256):
    tm = min(tm, seq)
    width = BRANCH_WIDTH
    nt = seq // tm
    w_ri = jnp.concatenate([w_r, w_i], axis=-1).astype(BF16)
    b_ri = jnp.stack([b_r, b_i])
    return pl.pallas_call(
        _lru_kernel,
        out_shape=jax.ShapeDtypeStruct((batch * seq, width), BF16),
        grid=(batch, nt),
        in_specs=[
            pl.BlockSpec((tm, width), lambda b, t: (b * nt + t, col["lx"] // width)),
            pl.BlockSpec((HALO_ROWS, width),
                         functools.partial(_halo_index(seq // HALO_ROWS, tm // HALO_ROWS),
                                           col=col["lx"] // width)),
            pl.BlockSpec((tm, width), lambda b, t: (b * nt + t, col["lz"] // width)),
            pl.BlockSpec((CONV_WIDTH, width), lambda b, t: (0, 0)),
            pl.BlockSpec((1, width), lambda b, t: (0, 0)),
            pl.BlockSpec((LRU_BLOCKS, LRU_BLOCK, 2 * LRU_BLOCK), lambda b, t: (0, 0, 0)),
            pl.BlockSpec((2, width), lambda b, t: (0, 0)),
            pl.BlockSpec((1, width), lambda b, t: (0, 0)),
        ],
        out_specs=pl.BlockSpec((tm, width), lambda b, t: (b * nt + t, 0)),
        scratch_shapes=[pltpu.VMEM((HALO_ROWS + tm, width), F32),
                        pltpu.VMEM((tm, width), F32),
                        pltpu.VMEM((tm, width), F32),
                        pltpu.VMEM((HALO_ROWS, width), F32)],
        compiler_params=_params(("parallel", "arbitrary"), VMEM_LIMIT),
        name="rglru",
    )(proj, proj, proj, conv_w, conv_b.reshape(1, width), w_ri, b_ri, lam.reshape(1, width))


def _s5_discretize_kernel(ldt_ref, are_ref, aim_ref, bre_ref, bim_ref,
                          bbre_ref, bbim_ref, pre_ref, pim_ref):
    dt = jnp.exp(ldt_ref[...])
    a_re = are_ref[...]
    a_im = aim_ref[...]
    mag = jnp.exp(dt * a_re)
    ab_re = mag * jnp.cos(dt * a_im)
    ab_im = mag * jnp.sin(dt * a_im)
    den = a_re * a_re + a_im * a_im
    f_re = ((ab_re - 1.0) * a_re + ab_im * a_im) / den
    f_im = (ab_im * a_re - (ab_re - 1.0) * a_im) / den
    b_re = bre_ref[...]
    b_im = bim_ref[...]
    bbre_ref[...] = f_re * b_re - f_im * b_im
    bbim_ref[...] = f_re * b_im + f_im * b_re
    p_re, p_im = ab_re, ab_im
    pre_ref[0] = p_re
    pim_ref[0] = p_im
    for p in range(1, HALO_ROWS):
        p_re, p_im = p_re * ab_re - p_im * ab_im, p_re * ab_im + p_im * ab_re
        pre_ref[p] = p_re
        pim_ref[p] = p_im


def _s5_discretize(log_dt, a_re, a_im, b_re, b_im):
    rows = SSM_GROUPS * SSM_GROUP
    rep = lambda p: jnp.repeat(p, SSM_GROUP, axis=0)
    ldt = rep(jnp.broadcast_to(log_dt[:, None], (SSM_GROUPS, SSM_STATE)))
    to_rows = lambda b: b.transpose(0, 2, 1).reshape(rows, SSM_STATE)
    shp = jax.ShapeDtypeStruct((rows, SSM_STATE), F32)
    pshp = jax.ShapeDtypeStruct((HALO_ROWS, rows, SSM_STATE), F32)
    bb_re, bb_im, p_re, p_im = pl.pallas_call(
        _s5_discretize_kernel, out_shape=(shp, shp, pshp, pshp), name="s5_discretize",
    )(ldt, rep(a_re), rep(a_im), to_rows(b_re), to_rows(b_im))
    return bb_re, bb_im, p_re[:, ::SSM_GROUP, :], p_im[:, ::SSM_GROUP, :]


def _s5_kernel(u_ref, z_ref, bw_ref, cw_ref, d_ref, s1_ref, s2_ref, s4_ref, pc_ref,
               wg_ref, bg_ref, o_ref, xs, y_s, carry):
    t = pl.program_id(1)
    tm = u_ref.shape[0]
    ns = SSM_BLOCK_STATES
    ch = SSM_GROUP * (SSM_GROUPS // SSM_LANE_BLOCKS)

    @pl.when(t == 0)
    def _():
        carry[...] = jnp.zeros(carry.shape, F32)

    for cb in range(SSM_LANE_BLOCKS):
        ub = u_ref[:, cb * ch:(cb + 1) * ch]
        xs[...] = _dot(ub, bw_ref[cb])
        steps = ((1, s1_ref), (2, s2_ref), (4, s4_ref))

        def group(g, c, cb=cb, steps=steps):
            rows = pl.ds(pl.multiple_of(g * HALO_ROWS, HALO_ROWS), HALO_ROWS)
            xr = xs[rows, 0:ns]
            xi = xs[rows, ns:2 * ns]
            for shift, s_ref in steps:
                mr = s_ref[cb, :, 0:ns]
                mi = s_ref[cb, :, ns:2 * ns]
                pr = pltpu.roll(xr, shift, 0)
                pi = pltpu.roll(xi, shift, 0)
                xr, xi = xr + (mr * pr - mi * pi), xi + (mr * pi + mi * pr)
            cr, ci = c
            mr = pc_ref[cb, :, 0:ns]
            mi = pc_ref[cb, :, ns:2 * ns]
            xr, xi = xr + (mr * cr - mi * ci), xi + (mr * ci + mi * cr)
            xs[rows, 0:ns] = xr
            xs[rows, ns:2 * ns] = xi
            last = slice(HALO_ROWS - 1, HALO_ROWS)
            return (jnp.broadcast_to(xr[last, :], (HALO_ROWS, ns)),
                    jnp.broadcast_to(xi[last, :], (HALO_ROWS, ns)))

        c0 = (carry[cb, :, 0:ns], carry[cb, :, ns:2 * ns])
        cr, ci = lax.fori_loop(0, tm // HALO_ROWS, group, c0)
        carry[cb, :, 0:ns] = cr
        carry[cb, :, ns:2 * ns] = ci
        y_s[:, cb * ch:(cb + 1) * ch] = _dot(xs[...], cw_ref[cb]) + d_ref[:, cb * ch:(cb + 1) * ch] * ub

    y = y_s[...]
    y = 0.5 * y * (1.0 + jnp.tanh(math.sqrt(2.0 / math.pi) * (y + 0.044715 * (y * y * y))))
    glu = _dot(y, wg_ref[...]) + bg_ref[...]
    width = y.shape[1]
    o_ref[...] = (glu[:, :width] * _sigmoid(glu[:, width:]) * _silu(z_ref[...])).astype(o_ref.dtype)


def _s5(proj, log_dt, a_re, a_im, b_re, b_im, c_re, c_im, d_skip, w_glu, b_glu, batch, seq, col, tm=256):
    tm = min(tm, seq)
    width = BRANCH_WIDTH
    nt = seq // tm
    nb = SSM_LANE_BLOCKS
    gb = SSM_GROUPS // nb
    ns = SSM_BLOCK_STATES
    bb_re, bb_im, p_re, p_im = _s5_discretize(log_dt, a_re, a_im, b_re, b_im)
    eye = jnp.eye(gb, dtype=F32)

    def in_block(bb):
        bb = bb.reshape(nb, gb, SSM_GROUP, SSM_STATE)
        return jnp.einsum("bgcn,gh->bgchn", bb, eye).reshape(nb, gb * SSM_GROUP, ns)

    def out_block(c):
        c = c.reshape(nb, gb, SSM_GROUP, SSM_STATE)
        return jnp.einsum("bgcn,gh->bgnhc", c, eye).reshape(nb, ns, gb * SSM_GROUP)

    bw = jnp.concatenate([in_block(bb_re), in_block(bb_im)], axis=-1).astype(BF16)
    cw = jnp.concatenate([out_block(c_re), -out_block(c_im)], axis=1).astype(BF16)

    def lanes(p):
        return p.reshape(p.shape[0], nb, ns).transpose(1, 0, 2)

    sub = jnp.arange(HALO_ROWS)[:, None, None]

    def step_mult(shift):
        pr = jnp.where(sub >= shift, p_re[shift - 1][None], 0.0)
        pi = jnp.where(sub >= shift, p_im[shift - 1][None], 0.0)
        return jnp.concatenate([lanes(pr), lanes(pi)], axis=-1)

    pc = jnp.concatenate([lanes(p_re), lanes(p_im)], axis=-1)
    mult_spec = pl.BlockSpec((nb, HALO_ROWS, 2 * ns), lambda b, t: (0, 0, 0))
    return pl.pallas_call(
        _s5_kernel,
        out_shape=jax.ShapeDtypeStruct((batch * seq, width), BF16),
        grid=(batch, nt),
        in_specs=[
            pl.BlockSpec((tm, width), lambda b, t: (b * nt + t, col["su"] // width)),
            pl.BlockSpec((tm, width), lambda b, t: (b * nt + t, col["sz"] // width)),
            pl.BlockSpec((nb, gb * SSM_GROUP, 2 * ns), lambda b, t: (0, 0, 0)),
            pl.BlockSpec((nb, 2 * ns, gb * SSM_GROUP), lambda b, t: (0, 0, 0)),
            pl.BlockSpec((1, width), lambda b, t: (0, 0)),
            mult_spec, mult_spec, mult_spec, mult_spec,
            pl.BlockSpec((width, 2 * width), lambda b, t: (0, 0)),
            pl.BlockSpec((1, 2 * width), lambda b, t: (0, 0)),
        ],
        out_specs=pl.BlockSpec((tm, width), lambda b, t: (b * nt + t, 0)),
        scratch_shapes=[pltpu.VMEM((tm, 2 * ns), F32),
                        pltpu.VMEM((tm, width), F32),
                        pltpu.VMEM((nb, HALO_ROWS, 2 * ns), F32)],
        compiler_params=_params(("parallel", "arbitrary"), VMEM_LIMIT),
        name="s5",
    )(proj, proj, bw, cw, d_skip.reshape(1, width), step_mult(1), step_mult(2), step_mult(4), pc,
      w_glu.astype(BF16), b_glu.reshape(1, 2 * width))


def _mem_attn_kernel(q_ref, z_ref, k_ref, v_ref, o_ref):
    dh = MEM_HEAD_DIM
    z = z_ref[...]
    for h in range(MEM_HEADS):
        cols = slice(h * dh, (h + 1) * dh)
        s = _dot_nt(q_ref[:, cols], k_ref[:, cols]) * (dh ** -0.5)
        s = s - jnp.max(s, axis=-1, keepdims=True)
        p = jnp.exp(s)
        p = p / jnp.sum(p, axis=-1, keepdims=True)
        o = _dot(p, v_ref[:, cols])
        o_ref[:, cols] = (o * _silu(z[:, cols])).astype(o_ref.dtype)


def _mem_attn(proj, kv, batch, seq, mem_len, col, tm=256):
    tm = min(tm, seq)
    width = BRANCH_WIDTH
    nt = seq // tm
    return pl.pallas_call(
        _mem_attn_kernel,
        out_shape=jax.ShapeDtypeStruct((batch * seq, width), BF16),
        grid=(batch, nt),
        in_specs=[
            pl.BlockSpec((tm, width), lambda b, t: (b * nt + t, col["mq"] // width)),
            pl.BlockSpec((tm, width), lambda b, t: (b * nt + t, col["mz"] // width)),
            pl.BlockSpec((mem_len, width), lambda b, t: (b, 0)),
            pl.BlockSpec((mem_len, width), lambda b, t: (b, 1)),
        ],
        out_specs=pl.BlockSpec((tm, width), lambda b, t: (b * nt + t, 0)),
        compiler_params=_params(("parallel", "parallel"), VMEM_LIMIT),
        name="mem_attn",
    )(proj, proj, kv, kv)


def _merge_kernel(g_ref, oa_ref, ob_ref, oc_ref, od_ref, wg_ref, bg_ref, wb_ref, o_ref):
    g_low = g_ref[...]
    acc = None
    for n, br_ref in enumerate((oa_ref, ob_ref, oc_ref, od_ref)):
        gate = _sigmoid(_dot(g_low, wg_ref[n]) + bg_ref[n:n + 1, :])
        term = gate * jnp.dot(br_ref[...], wb_ref[n], preferred_element_type=F32)
        acc = term if acc is None else acc + term
    o_ref[...] = acc.astype(o_ref.dtype)


def _merge(proj, branches, w_gate, b_gate, w_branch, col, tm=1024, tn=512):
    m = proj.shape[0]
    d_model = w_gate.shape[-1]
    tm = min(tm, m)
    br_spec = pl.BlockSpec((tm, BRANCH_WIDTH), lambda i, j: (i, 0))
    return pl.pallas_call(
        _merge_kernel,
        out_shape=jax.ShapeDtypeStruct((m, d_model), BF16),
        grid=(m // tm, d_model // tn),
        in_specs=[
            pl.BlockSpec((tm, GATE_RANK), lambda i, j: (i, col["glow"] // GATE_RANK)),
            br_spec, br_spec, br_spec, br_spec,
            pl.BlockSpec((N_BRANCH, GATE_RANK, tn), lambda i, j: (0, 0, j)),
            pl.BlockSpec((N_BRANCH, tn), lambda i, j: (0, j)),
            pl.BlockSpec((N_BRANCH, BRANCH_WIDTH, tn), lambda i, j: (0, 0, j)),
        ],
        out_specs=pl.BlockSpec((tm, tn), lambda i, j: (i, j)),
        compiler_params=_params(("parallel", "arbitrary"), VMEM_LIMIT),
        name="merge",
    )(proj, *branches, w_gate.astype(BF16), b_gate, w_branch.astype(BF16))


def _in_proj_layout(w_in_l):
    d_model = w_in_l.shape[0]
    a_end = 4 * BRANCH_WIDTH
    small = 2 * DN_HEADS
    main = 6 * BRANCH_WIDTH + GATE_RANK
    pad = (-(a_end + main + small)) % 512
    w = jnp.concatenate([w_in_l[:, :a_end], w_in_l[:, a_end + small:], w_in_l[:, a_end:a_end + small],
                         jnp.zeros((d_model, pad), w_in_l.dtype)], axis=1).astype(BF16)
    w1 = BRANCH_WIDTH
    col = {"qkv": 0, "dz": 3 * w1, "lx": 4 * w1, "lz": 5 * w1, "su": 6 * w1, "sz": 7 * w1,
           "mq": 8 * w1, "mz": 9 * w1, "glow": 10 * w1, "small": 10 * w1 + GATE_RANK}
    return w, col


def kernel(x, mem, norm_w, w_in, dn_conv_w, dn_a_log, dn_dt_bias, dn_norm_w, lru_conv_w, lru_conv_b, lru_w_r, lru_b_r, lru_w_i, lru_b_i, lru_lambda, ssm_log_dt, ssm_a_re, ssm_a_im, ssm_b_re, ssm_b_im, ssm_c_re, ssm_c_im, ssm_d, ssm_w_glu, ssm_b_glu, mem_norm_w, w_kv, w_gate, b_gate, w_branch, w_out, final_norm_w):
    batch, seq, d_model = x.shape
    mem_len = mem.shape[1]
    depth = w_in.shape[0]
    xf = x.reshape(batch * seq, d_model)
    memf = mem.reshape(batch * mem_len, d_model)
    for l in range(depth):
        w_cat, col = _in_proj_layout(w_in[l])
        h = _rmsnorm(xf, norm_w[l], BF16)
        proj = _matmul(h, w_cat, F32)
        o_a = _deltanet(proj, dn_conv_w[l], dn_a_log[l], dn_dt_bias[l], dn_norm_w[l], batch, seq, col)
        o_b = _rglru(proj, lru_conv_w[l], lru_conv_b[l], lru_w_r[l], lru_b_r[l], lru_w_i[l], lru_b_i[l],
                     lru_lambda[l], batch, seq, col)
        o_c = _s5(proj, ssm_log_dt[l], ssm_a_re[l], ssm_a_im[l], ssm_b_re[l], ssm_b_im[l],
                  ssm_c_re[l], ssm_c_im[l], ssm_d[l].reshape(-1), ssm_w_glu[l], ssm_b_glu[l], batch, seq, col)
        m_n = _rmsnorm(memf, mem_norm_w[l], BF16)
        kv = _matmul(m_n, w_kv[l].astype(BF16), BF16)
        o_d = _mem_attn(proj, kv, batch, seq, mem_len, col)
        merged = _merge(proj, (o_a, o_b, o_c, o_d), w_gate[l], b_gate[l], w_branch[l], col)
        xf = _matmul(merged, w_out[l].astype(BF16), F32, residual=xf)
    out = _rmsnorm(xf, final_norm_w, F32)
    return out.reshape(batch, seq, d_model)
```

```python
import functools
import math

import jax
import jax.numpy as jnp
from jax import lax
from jax.experimental import pallas as pl
from jax.experimental.pallas import tpu as pltpu

F32 = jnp.float32
BF16 = jnp.bfloat16
HIGHEST = lax.Precision.HIGHEST

NORM_EPS = 1e-6
CONV_WIDTH = 4
HALO_ROWS = 8
N_BRANCH = 4
BRANCH_WIDTH = 1024
DN_HEAD_DIM = 128
DN_HEADS = 8
DN_CHUNK = 64
DN_TILE = 128
LRU_BLOCKS = 8
LRU_BLOCK = 128
LRU_C = 8.0
SSM_GROUP = 16
SSM_GROUPS = 64
SSM_STATE = 64
SSM_LANE_BLOCKS = 8
SSM_BLOCK_STATES = 512
MEM_HEADS = 4
MEM_HEAD_DIM = 256
GATE_RANK = 256
SMALL_COLS = 128

VMEM_LIMIT = 56 * 1024 * 1024


def _params(sem, vmem=None):
    return pltpu.CompilerParams(dimension_semantics=sem, vmem_limit_bytes=vmem)


def _sigmoid(x):
    return 1.0 / (1.0 + jnp.exp(-x))


def _silu(x):
    return x * _sigmoid(x)


def _softplus(x):
    return jnp.maximum(x, 0.0) + jnp.log1p(jnp.exp(-jnp.abs(x)))


def _dot(a, b):
    return jnp.dot(a.astype(BF16), b.astype(BF16), preferred_element_type=F32)


def _dot_nt(a, b):
    return lax.dot_general(a.astype(BF16), b.astype(BF16), (((1,), (1,)), ((), ())),
                           preferred_element_type=F32)


def _dot_f32(a, b):
    return jnp.dot(a, b, precision=HIGHEST, preferred_element_type=F32)


def _rmsnorm_kernel(x_ref, w_ref, o_ref):
    x = x_ref[...]
    var = jnp.mean(x * x, axis=-1, keepdims=True)
    o_ref[...] = (x * lax.rsqrt(var + NORM_EPS) * w_ref[...]).astype(o_ref.dtype)


def _rmsnorm(x, w, out_dtype, tm=256):
    m, d = x.shape
    tm = min(tm, m)
    return pl.pallas_call(
        _rmsnorm_kernel,
        out_shape=jax.ShapeDtypeStruct((m, d), out_dtype),
        grid=(m // tm,),
        in_specs=[pl.BlockSpec((tm, d), lambda i: (i, 0)),
                  pl.BlockSpec((1, d), lambda i: (0, 0))],
        out_specs=pl.BlockSpec((tm, d), lambda i: (i, 0)),
        compiler_params=_params(("parallel",)),
        name="rmsnorm",
    )(x, w.reshape(1, d))


def _mm_kernel(a_ref, w_ref, o_ref):
    o_ref[...] = jnp.dot(a_ref[...], w_ref[...], preferred_element_type=F32).astype(o_ref.dtype)


def _matmul(a, w, layer, out_dtype, tm=1024, tn=512):
    m, k = a.shape
    n = w.shape[2]
    tm = min(tm, m)
    tn = min(tn, n)
    return pl.pallas_call(
        _mm_kernel,
        out_shape=jax.ShapeDtypeStruct((m, n), out_dtype),
        grid=(m // tm, n // tn),
        in_specs=[pl.BlockSpec((tm, k), lambda i, j: (i, 0)),
                  pl.BlockSpec((None, k, tn), lambda i, j: (layer, 0, j))],
        out_specs=pl.BlockSpec((tm, tn), lambda i, j: (i, j)),
        compiler_params=_params(("parallel", "arbitrary"), VMEM_LIMIT),
        name="matmul",
    )(a, w)


def _out_norm_kernel(a_ref, w_ref, r_ref, nw_ref, *refs, write_x):
    if write_x:
        x_ref, h_ref, acc = refs
    else:
        h_ref, acc = refs
    j = pl.program_id(1)
    nj, _, tn = acc.shape
    acc[j] = r_ref[...] + jnp.dot(a_ref[...], w_ref[...], preferred_element_type=F32)

    @pl.when(j == nj - 1)
    def _():
        ssq = None
        for jj in range(nj):
            blk = acc[jj]
            part = jnp.sum(blk * blk, axis=-1, keepdims=True)
            ssq = part if ssq is None else ssq + part
        inv = lax.rsqrt(ssq * (1.0 / (nj * tn)) + NORM_EPS)
        for jj in range(nj):
            blk = acc[jj]
            cols = slice(jj * tn, (jj + 1) * tn)
            if write_x:
                x_ref[:, cols] = blk
            h_ref[:, cols] = (blk * inv * nw_ref[:, cols]).astype(h_ref.dtype)


def _out_proj_norm(a, w, layer, residual, norm_w, write_x, h_dtype, tm=512, tn=512):
    m, k = a.shape
    n = w.shape[2]
    tm = min(tm, m)
    row_spec = pl.BlockSpec((tm, n), lambda i, j: (i, 0))
    out_shape = [jax.ShapeDtypeStruct((m, n), h_dtype)]
    out_specs = [row_spec]
    if write_x:
        out_shape.insert(0, jax.ShapeDtypeStruct((m, n), F32))
        out_specs.insert(0, row_spec)
    outs = pl.pallas_call(
        functools.partial(_out_norm_kernel, write_x=write_x),
        out_shape=out_shape,
        grid=(m // tm, n // tn),
        in_specs=[pl.BlockSpec((tm, k), lambda i, j: (i, 0)),
                  pl.BlockSpec((None, k, tn), lambda i, j: (layer, 0, j)),
                  pl.BlockSpec((tm, tn), lambda i, j: (i, j)),
                  pl.BlockSpec((1, n), lambda i, j: (0, 0))],
        out_specs=out_specs,
        scratch_shapes=[pltpu.VMEM((n // tn, tm, tn), F32)],
        compiler_params=_params(("parallel", "arbitrary"), VMEM_LIMIT),
        name="out_proj_norm",
    )(a, w, residual, norm_w.reshape(1, n))
    return (outs[0], outs[1]) if write_x else (None, outs[0])


def _fill_conv_buffer(buf, halo_ref, x_ref, t, tm):
    @pl.when(t == 0)
    def _():
        buf[0:HALO_ROWS, :] = jnp.zeros((HALO_ROWS, buf.shape[1]), F32)

    @pl.when(t != 0)
    def _():
        buf[0:HALO_ROWS, :] = halo_ref[...]

    buf[HALO_ROWS:HALO_ROWS + tm, :] = x_ref[...]


def _causal_conv(buf, cw_ref, tm):
    first = HALO_ROWS - (CONV_WIDTH - 1)
    last = CONV_WIDTH - 1
    acc = cw_ref[last:last + 1, :] * buf[first + last:first + last + tm, :]
    for j in range(last - 1, -1, -1):
        acc = acc + cw_ref[j:j + 1, :] * buf[first + j:first + j + tm, :]
    return acc


def _halo_index(seq_blocks, tile_blocks):
    def index(b, t, col):
        return (jnp.maximum(b * seq_blocks + t * tile_blocks - 1, 0), col)
    return index


def _dn_kernel(qkv_ref, halo_ref, z_ref, sm_ref, cw_ref, alog_ref, dtb_ref, nw_ref, o_ref,
               buf, state):
    t = pl.program_id(1)
    tm = DN_TILE
    d = DN_HEAD_DIM

    @pl.when(t == 0)
    def _():
        state[...] = jnp.zeros(state.shape, F32)

    _fill_conv_buffer(buf, halo_ref, qkv_ref, t, tm)
    qkv = _silu(_causal_conv(buf, cw_ref, tm))

    row = lax.broadcasted_iota(jnp.int32, (tm, tm), 0)
    col = lax.broadcasted_iota(jnp.int32, (tm, tm), 1)
    same_chunk = (row // DN_CHUNK) == (col // DN_CHUNK)
    causal = same_chunk & (row >= col)
    strict = same_chunk & (row > col)

    small = sm_ref[...]
    beta_all = _sigmoid(small)
    g_all = -jnp.exp(alog_ref[...]) * _softplus(small + dtb_ref[...])
    gc_all = _dot_f32(causal.astype(F32), g_all)
    gl_all = _dot_f32(same_chunk.astype(F32), g_all)
    gc_rows = gc_all.T

    z = z_ref[...]
    nw = nw_ref[...]
    heads = range(DN_HEADS)
    chunks = range(tm // DN_CHUNK)
    zero_chunk = jnp.zeros((DN_CHUNK, 2 * d), F32)

    q, k, kb, qd, kd_t, decay, sol, gl = [], [], [], [], [], [], [], []
    for h in heads:
        qh = qkv[:, h * d:(h + 1) * d]
        kh = qkv[:, BRANCH_WIDTH + h * d:BRANCH_WIDTH + (h + 1) * d]
        vh = qkv[:, 2 * BRANCH_WIDTH + h * d:2 * BRANCH_WIDTH + (h + 1) * d]
        qh = qh * lax.rsqrt(jnp.sum(qh * qh, axis=-1, keepdims=True) + NORM_EPS) * (d ** -0.5)
        kh = kh * lax.rsqrt(jnp.sum(kh * kh, axis=-1, keepdims=True) + NORM_EPS)
        beta = beta_all[:, h:h + 1]
        gc = gc_all[:, DN_HEADS + h:DN_HEADS + h + 1]
        glh = gl_all[:, DN_HEADS + h:DN_HEADS + h + 1]
        gr = gc_rows[DN_HEADS + h:DN_HEADS + h + 1, :]
        decay.append(jnp.where(causal, jnp.exp(jnp.where(causal, gc - gr, 0.0)), 0.0))
        eg = jnp.exp(gc)
        kbh = kh * beta
        q.append(qh.astype(BF16))
        k.append(kh.astype(BF16))
        kb.append(kbh.astype(BF16))
        qd.append(qh * eg)
        kd_t.append((kh * jnp.exp(glh - gc)).T.astype(BF16))
        sol.append(jnp.concatenate([vh * beta, kbh * eg], axis=-1))
        gl.append(glh)

    x, qk = [], []
    for h in heads:
        both = lax.dot_general(jnp.concatenate([kb[h], q[h]], axis=0), k[h], (((1,), (1,)), ((), ())),
                               preferred_element_type=F32)
        x.append(jnp.where(strict, -(both[:tm] * decay[h]), 0.0))
        qk.append(jnp.where(causal, both[tm:] * decay[h], 0.0).astype(BF16))

    n_square = 5
    for it in range(n_square + 1):
        for h in heads:
            xb = x[h].astype(BF16)
            sb = sol[h].astype(BF16)
            if it < n_square:
                y = jnp.dot(xb, jnp.concatenate([xb, sb], axis=-1), preferred_element_type=F32)
                x[h] = y[:, :tm]
                sol[h] = sol[h] + y[:, tm:]
            else:
                sol[h] = sol[h] + jnp.dot(xb, sb, preferred_element_type=F32)

    kw, ku = [], []
    for h in heads:
        kw_h, ku_h = [], []
        for c in chunks:
            rows = slice(c * DN_CHUNK, (c + 1) * DN_CHUNK)
            wu = jnp.concatenate([sol[h][rows, d:], sol[h][rows, :d]], axis=-1)
            padded = jnp.concatenate([wu if i == c else zero_chunk for i in chunks], axis=0)
            prod = jnp.dot(kd_t[h], padded.astype(BF16), preferred_element_type=F32)
            kw_h.append(prod[:, :d].astype(BF16))
            ku_h.append(prod[:, d:])
        kw.append(kw_h)
        ku.append(ku_h)

    s = [state[h] for h in heads]
    inter = [[] for _ in heads]
    v_new = [[] for _ in heads]
    for c in chunks:
        rows = slice(c * DN_CHUNK, (c + 1) * DN_CHUNK)
        for h in heads:
            sb = s[h].astype(BF16)
            lhs = jnp.concatenate([qd[h][rows], sol[h][rows, d:]], axis=0).astype(BF16)
            r = jnp.dot(lhs, sb, preferred_element_type=F32)
            inter[h].append(r[:DN_CHUNK])
            v_new[h].append(sol[h][rows, :d] - r[DN_CHUNK:])
            s[h] = (s[h] * jnp.exp(gl[h][c * DN_CHUNK:c * DN_CHUNK + 1, :])
                    - jnp.dot(kw[h][c], sb, preferred_element_type=F32) + ku[h][c])
    for h in heads:
        state[h] = s[h]
        out = jnp.concatenate(inter[h], axis=0) + jnp.dot(
            qk[h], jnp.concatenate(v_new[h], axis=0).astype(BF16), preferred_element_type=F32)
        var = jnp.mean(out * out, axis=-1, keepdims=True)
        out = out * lax.rsqrt(var + NORM_EPS) * nw
        o_ref[:, h * d:(h + 1) * d] = (out * _silu(z[:, h * d:(h + 1) * d])).astype(o_ref.dtype)


def _deltanet(proj, conv_w, a_log, dt_bias, norm_w, batch, seq, col):
    tm = DN_TILE
    width = 3 * BRANCH_WIDTH
    pad = (0, SMALL_COLS - 2 * DN_HEADS)
    alog = jnp.pad(jnp.concatenate([jnp.zeros((DN_HEADS,), F32), a_log]), pad).reshape(1, SMALL_COLS)
    dtb = jnp.pad(jnp.concatenate([jnp.zeros((DN_HEADS,), F32), dt_bias]), pad).reshape(1, SMALL_COLS)
    nt = seq // tm
    return pl.pallas_call(
        _dn_kernel,
        out_shape=jax.ShapeDtypeStruct((batch * seq, BRANCH_WIDTH), BF16),
        grid=(batch, nt),
        in_specs=[
            pl.BlockSpec((tm, width), lambda b, t: (b * nt + t, col["qkv"] // width)),
            pl.BlockSpec((HALO_ROWS, width),
                         functools.partial(_halo_index(seq // HALO_ROWS, tm // HALO_ROWS),
                                           col=col["qkv"] // width)),
            pl.BlockSpec((tm, BRANCH_WIDTH), lambda b, t: (b * nt + t, col["dz"] // BRANCH_WIDTH)),
            pl.BlockSpec((tm, SMALL_COLS), lambda b, t: (b * nt + t, col["small"] // SMALL_COLS)),
            pl.BlockSpec((CONV_WIDTH, width), lambda b, t: (0, 0)),
            pl.BlockSpec((1, SMALL_COLS), lambda b, t: (0, 0)),
            pl.BlockSpec((1, SMALL_COLS), lambda b, t: (0, 0)),
            pl.BlockSpec((1, DN_HEAD_DIM), lambda b, t: (0, 0)),
        ],
        out_specs=pl.BlockSpec((tm, BRANCH_WIDTH), lambda b, t: (b * nt + t, 0)),
        scratch_shapes=[pltpu.VMEM((HALO_ROWS + tm, width), F32),
                        pltpu.VMEM((DN_HEADS, DN_HEAD_DIM, DN_HEAD_DIM), F32)],
        compiler_params=_params(("parallel", "arbitrary"), VMEM_LIMIT),
        name="deltanet",
    )(proj, proj, proj, proj, conv_w, alog, dtb, norm_w.reshape(1, DN_HEAD_DIM))


def _lru_kernel(x_ref, halo_ref, z_ref, cw_ref, cb_ref, w_ref, b_ref, lam_ref, o_ref,
                buf, a_s, b_s, carry):
    t = pl.program_id(1)
    tm = x_ref.shape[0]
    width = x_ref.shape[1]

    @pl.when(t == 0)
    def _():
        carry[...] = jnp.zeros(carry.shape, F32)

    _fill_conv_buffer(buf, halo_ref, x_ref, t, tm)
    xc = _causal_conv(buf, cw_ref, tm) + cb_ref[...]
    gates = []
    for n in range(LRU_BLOCKS):
        gates.append(_dot(xc[:, n * LRU_BLOCK:(n + 1) * LRU_BLOCK], w_ref[n]))
    r = jnp.concatenate([g[:, :LRU_BLOCK] for g in gates], axis=-1)
    i = jnp.concatenate([g[:, LRU_BLOCK:] for g in gates], axis=-1)
    r = _sigmoid(r + b_ref[0:1, :])
    i = _sigmoid(i + b_ref[1:2, :])
    log_a = -LRU_C * r * _softplus(-lam_ref[...])
    a_s[...] = jnp.exp(log_a)
    one_minus_a2 = 1.0 - jnp.exp(2.0 * log_a)
    root = jnp.where(one_minus_a2 > 0.0, one_minus_a2 * lax.rsqrt(one_minus_a2), 0.0)
    b_s[...] = root * (i * xc)

    sub = lax.broadcasted_iota(jnp.int32, (HALO_ROWS, width), 0)

    def group(g, h_prev):
        rows = pl.ds(pl.multiple_of(g * HALO_ROWS, HALO_ROWS), HALO_ROWS)
        a = a_s[rows, :]
        b = b_s[rows, :]
        for shift in (1, 2, 4):
            keep = sub >= shift
            a_prev = jnp.where(keep, pltpu.roll(a, shift, 0), 1.0)
            b_prev = jnp.where(keep, pltpu.roll(b, shift, 0), 0.0)
            b = a * b_prev + b
            a = a * a_prev
        h = a * h_prev + b
        b_s[rows, :] = h
        return jnp.broadcast_to(h[HALO_ROWS - 1:HALO_ROWS, :], (HALO_ROWS, width))

    carry[...] = lax.fori_loop(0, tm // HALO_ROWS, group, carry[...])
    o_ref[...] = (b_s[...] * _silu(z_ref[...])).astype(o_ref.dtype)


def _rglru(proj, conv_w, conv_b, w_ri, layer, b_r, b_i, lam, batch, seq, col, tm=256):
    tm = min(tm, seq)
    width = BRANCH_WIDTH
    nt = seq // tm
    b_ri = jnp.stack([b_r, b_i])
    return pl.pallas_call(
        _lru_kernel,
        out_shape=jax.ShapeDtypeStruct((batch * seq, width), BF16),
        grid=(batch, nt),
        in_specs=[
            pl.BlockSpec((tm, width), lambda b, t: (b * nt + t, col["lx"] // width)),
            pl.BlockSpec((HALO_ROWS, width),
                         functools.partial(_halo_index(seq // HALO_ROWS, tm // HALO_ROWS),
                                           col=col["lx"] // width)),
            pl.BlockSpec((tm, width), lambda b, t: (b * nt + t, col["lz"] // width)),
            pl.BlockSpec((CONV_WIDTH, width), lambda b, t: (0, 0)),
            pl.BlockSpec((1, width), lambda b, t: (0, 0)),
            pl.BlockSpec((None, LRU_BLOCKS, LRU_BLOCK, 2 * LRU_BLOCK), lambda b, t: (layer, 0, 0, 0)),
            pl.BlockSpec((2, width), lambda b, t: (0, 0)),
            pl.BlockSpec((1, width), lambda b, t: (0, 0)),
        ],
        out_specs=pl.BlockSpec((tm, width), lambda b, t: (b * nt + t, 0)),
        scratch_shapes=[pltpu.VMEM((HALO_ROWS + tm, width), F32),
                        pltpu.VMEM((tm, width), F32),
                        pltpu.VMEM((tm, width), F32),
                        pltpu.VMEM((HALO_ROWS, width), F32)],
        compiler_params=_params(("parallel", "arbitrary"), VMEM_LIMIT),
        name="rglru",
    )(proj, proj, proj, conv_w, conv_b.reshape(1, width), w_ri, b_ri, lam.reshape(1, width))


def _zoh(ldt, a_re, a_im):
    dt = jnp.exp(ldt)
    mag = jnp.exp(dt * a_re)
    return mag * jnp.cos(dt * a_im), mag * jnp.sin(dt * a_im)


def _s5_discretize_kernel(ldt_ref, are_ref, aim_ref, bre_ref, bim_ref, ldtg_ref, areg_ref, aimg_ref,
                          bbre_ref, bbim_ref, pre_ref, pim_ref):
    a_re = are_ref[...]
    a_im = aim_ref[...]
    ab_re, ab_im = _zoh(ldt_ref[...], a_re, a_im)
    den = a_re * a_re + a_im * a_im
    f_re = ((ab_re - 1.0) * a_re + ab_im * a_im) / den
    f_im = (ab_im * a_re - (ab_re - 1.0) * a_im) / den
    b_re = bre_ref[...]
    b_im = bim_ref[...]
    bbre_ref[...] = f_re * b_re - f_im * b_im
    bbim_ref[...] = f_re * b_im + f_im * b_re

    seg_len = pre_ref.shape[0] - 2
    ab_re, ab_im = _zoh(ldtg_ref[...], areg_ref[...], aimg_ref[...])
    p_re, p_im = ab_re, ab_im
    pre_ref[0] = p_re
    pim_ref[0] = p_im
    for p in range(1, seg_len):
        p_re, p_im = p_re * ab_re - p_im * ab_im, p_re * ab_im + p_im * ab_re
        pre_ref[p] = p_re
        pim_ref[p] = p_im
    for p in range(seg_len, seg_len + 2):
        p_re, p_im = p_re * p_re - p_im * p_im, 2.0 * (p_re * p_im)
        pre_ref[p] = p_re
        pim_ref[p] = p_im


def _s5_discretize(log_dt, a_re, a_im, b_re, b_im, seg_len):
    rows = SSM_GROUPS * SSM_GROUP
    rep = lambda p: jnp.repeat(p, SSM_GROUP, axis=0)
    ldt = jnp.broadcast_to(log_dt[:, None], (SSM_GROUPS, SSM_STATE))
    to_rows = lambda b: b.transpose(0, 2, 1).reshape(rows, SSM_STATE)
    shp = jax.ShapeDtypeStruct((rows, SSM_STATE), F32)
    pshp = jax.ShapeDtypeStruct((seg_len + 2, SSM_GROUPS, SSM_STATE), F32)
    return pl.pallas_call(
        _s5_discretize_kernel, out_shape=(shp, shp, pshp, pshp), name="s5_discretize",
    )(rep(ldt), rep(a_re), rep(a_im), to_rows(b_re), to_rows(b_im), ldt, a_re, a_im)


def _s5_kernel(u_ref, z_ref, bw_ref, cw_ref, d_ref, a1_ref, hs_ref, al_ref,
               wg_ref, bg_ref, o_ref, up, xs, y_s, carry):
    t = pl.program_id(1)
    tm = u_ref.shape[0]
    seg = tm // HALO_ROWS
    ns = SSM_BLOCK_STATES
    ch = SSM_GROUP * (SSM_GROUPS // SSM_LANE_BLOCKS)

    @pl.when(t == 0)
    def _():
        carry[...] = jnp.zeros(carry.shape, F32)

    wrow = lax.broadcasted_iota(jnp.int32, (tm, tm), 0)
    trow = lax.broadcasted_iota(jnp.int32, (tm, tm), 1)
    select = (trow == (wrow % HALO_ROWS) * seg + wrow // HALO_ROWS).astype(BF16)
    up[...] = jnp.dot(select, u_ref[...].astype(BF16), preferred_element_type=F32).astype(BF16)

    sub = lax.broadcasted_iota(jnp.int32, (HALO_ROWS, ns), 0)
    zero = jnp.zeros((HALO_ROWS, ns), F32)
    def input_states(cb):
        xs[cb % 2] = jnp.dot(up[:, cb * ch:(cb + 1) * ch], bw_ref[cb], preferred_element_type=F32)

    input_states(0)
    for cb in range(SSM_LANE_BLOCKS):
        if cb + 1 < SSM_LANE_BLOCKS:
            input_states(cb + 1)
        xb = xs.at[cb % 2]
        ar = a1_ref[cb, :, 0:ns]
        ai = a1_ref[cb, :, ns:2 * ns]
        er, ei = zero, zero
        for i in range(seg):
            rows = slice(i * HALO_ROWS, (i + 1) * HALO_ROWS)
            er, ei = (xb[rows, 0:ns] + (ar * er - ai * ei),
                      xb[rows, ns:2 * ns] + (ar * ei + ai * er))

        fr = jnp.where(sub == 0, carry[cb, :, 0:ns], pltpu.roll(er, 1, 0))
        fi = jnp.where(sub == 0, carry[cb, :, ns:2 * ns], pltpu.roll(ei, 1, 0))
        for k, shift in enumerate((1, 2, 4)):
            mr = hs_ref[k, cb, :, 0:ns]
            mi = hs_ref[k, cb, :, ns:2 * ns]
            pr = pltpu.roll(fr, shift, 0)
            pi = pltpu.roll(fi, shift, 0)
            fr, fi = fr + (mr * pr - mi * pi), fi + (mr * pi + mi * pr)
        lr = al_ref[cb, :, 0:ns]
        li = al_ref[cb, :, ns:2 * ns]
        last = slice(HALO_ROWS - 1, HALO_ROWS)
        carry[cb, :, 0:ns] = jnp.broadcast_to((lr * fr - li * fi + er)[last, :], (HALO_ROWS, ns))
        carry[cb, :, ns:2 * ns] = jnp.broadcast_to((lr * fi + li * fr + ei)[last, :], (HALO_ROWS, ns))

        for i in range(seg):
            rows = slice(i * HALO_ROWS, (i + 1) * HALO_ROWS)
            fr, fi = (xb[rows, 0:ns] + (ar * fr - ai * fi),
                      xb[rows, ns:2 * ns] + (ar * fi + ai * fr))
            xb[rows, 0:ns] = fr
            xb[rows, ns:2 * ns] = fi
        cx = _dot(xb[...], cw_ref[cb])
        for i in range(seg):
            y_s[cb, pl.ds(i, HALO_ROWS, stride=seg), :] = cx[i * HALO_ROWS:(i + 1) * HALO_ROWS, :]

    y = jnp.concatenate([y_s[cb] for cb in range(SSM_LANE_BLOCKS)], axis=-1) + d_ref[...] * u_ref[...]
    y = 0.5 * y * (1.0 + jnp.tanh(math.sqrt(2.0 / math.pi) * (y + 0.044715 * (y * y * y))))
    glu = _dot(y, wg_ref[...]) + bg_ref[...]
    width = y.shape[1]
    o_ref[...] = (glu[:, :width] * _sigmoid(glu[:, width:]) * _silu(z_ref[...])).astype(o_ref.dtype)


def _s5(proj, log_dt, a_re, a_im, b_re, b_im, c_re, c_im, d_skip, w_glu, layer, b_glu, batch, seq, col, tm=256):
    tm = min(tm, seq)
    width = BRANCH_WIDTH
    nt = seq // tm
    seg = tm // HALO_ROWS
    nb = SSM_LANE_BLOCKS
    gb = SSM_GROUPS // nb
    ns = SSM_BLOCK_STATES
    bb_re, bb_im, p_re, p_im = _s5_discretize(log_dt, a_re, a_im, b_re, b_im, seg)
    eye = jnp.eye(gb, dtype=F32)

    def in_block(bb):
        bb = bb.reshape(nb, gb, SSM_GROUP, SSM_STATE)
        return jnp.einsum("bgcn,gh->bgchn", bb, eye).reshape(nb, gb * SSM_GROUP, ns)

    def out_block(c):
        c = c.reshape(nb, gb, SSM_GROUP, SSM_STATE)
        return jnp.einsum("bgcn,gh->bgnhc", c, eye).reshape(nb, ns, gb * SSM_GROUP)

    bw = jnp.concatenate([in_block(bb_re), in_block(bb_im)], axis=-1).astype(BF16)
    cw = jnp.concatenate([out_block(c_re), -out_block(c_im)], axis=1).astype(BF16)

    def lanes(p):
        return p.reshape(p.shape[0], nb, ns).transpose(1, 0, 2)

    sub = jnp.arange(HALO_ROWS)[:, None, None]

    def rows8(p_re_1, p_im_1, first_row=0):
        pr = jnp.where(sub >= first_row, p_re_1[None], 0.0)
        pi = jnp.where(sub >= first_row, p_im_1[None], 0.0)
        return jnp.concatenate([lanes(pr), lanes(pi)], axis=-1)

    a1 = rows8(p_re[0], p_im[0])
    al = rows8(p_re[seg - 1], p_im[seg - 1])
    hs = jnp.stack([rows8(p_re[seg - 1], p_im[seg - 1], 1), rows8(p_re[seg], p_im[seg], 2),
                    rows8(p_re[seg + 1], p_im[seg + 1], 4)])
    mult_spec = pl.BlockSpec((nb, HALO_ROWS, 2 * ns), lambda b, t: (0, 0, 0))
    return pl.pallas_call(
        _s5_kernel,
        out_shape=jax.ShapeDtypeStruct((batch * seq, width), BF16),
        grid=(batch, nt),
        in_specs=[
            pl.BlockSpec((tm, width), lambda b, t: (b * nt + t, col["su"] // width)),
            pl.BlockSpec((tm, width), lambda b, t: (b * nt + t, col["sz"] // width)),
            pl.BlockSpec((nb, gb * SSM_GROUP, 2 * ns), lambda b, t: (0, 0, 0)),
            pl.BlockSpec((nb, 2 * ns, gb * SSM_GROUP), lambda b, t: (0, 0, 0)),
            pl.BlockSpec((1, width), lambda b, t: (0, 0)),
            mult_spec,
            pl.BlockSpec((3, nb, HALO_ROWS, 2 * ns), lambda b, t: (0, 0, 0, 0)),
            mult_spec,
            pl.BlockSpec((None, width, 2 * width), lambda b, t: (layer, 0, 0)),
            pl.BlockSpec((1, 2 * width), lambda b, t: (0, 0)),
        ],
        out_specs=pl.BlockSpec((tm, width), lambda b, t: (b * nt + t, 0)),
        scratch_shapes=[pltpu.VMEM((tm, width), BF16),
                        pltpu.VMEM((2, tm, 2 * ns), F32),
                        pltpu.VMEM((nb, tm, gb * SSM_GROUP), F32),
                        pltpu.VMEM((nb, HALO_ROWS, 2 * ns), F32)],
        compiler_params=_params(("parallel", "arbitrary"), VMEM_LIMIT),
        name="s5",
    )(proj, proj, bw, cw, d_skip.reshape(1, width), a1, hs, al, w_glu, b_glu.reshape(1, 2 * width))


def _mem_attn_kernel(q_ref, z_ref, k_ref, v_ref, o_ref):
    dh = MEM_HEAD_DIM
    z = z_ref[...]
    for h in range(MEM_HEADS):
        cols = slice(h * dh, (h + 1) * dh)
        s = _dot_nt(q_ref[:, cols], k_ref[:, cols]) * (dh ** -0.5)
        s = s - jnp.max(s, axis=-1, keepdims=True)
        p = jnp.exp(s)
        p = p / jnp.sum(p, axis=-1, keepdims=True)
        o = _dot(p, v_ref[:, cols])
        o_ref[:, cols] = (o * _silu(z[:, cols])).astype(o_ref.dtype)


def _mem_attn(proj, kv, batch, seq, mem_len, col, tm=256):
    tm = min(tm, seq)
    width = BRANCH_WIDTH
    nt = seq // tm
    return pl.pallas_call(
        _mem_attn_kernel,
        out_shape=jax.ShapeDtypeStruct((batch * seq, width), BF16),
        grid=(batch, nt),
        in_specs=[
            pl.BlockSpec((tm, width), lambda b, t: (b * nt + t, col["mq"] // width)),
            pl.BlockSpec((tm, width), lambda b, t: (b * nt + t, col["mz"] // width)),
            pl.BlockSpec((mem_len, width), lambda b, t: (b, 0)),
            pl.BlockSpec((mem_len, width), lambda b, t: (b, 1)),
        ],
        out_specs=pl.BlockSpec((tm, width), lambda b, t: (b * nt + t, 0)),
        compiler_params=_params(("parallel", "parallel"), VMEM_LIMIT),
        name="mem_attn",
    )(proj, proj, kv, kv)


def _merge_kernel(g_ref, oa_ref, ob_ref, oc_ref, od_ref, wg_ref, bg_ref, wb_ref, o_ref):
    g_low = g_ref[...]
    acc = None
    for n, br_ref in enumerate((oa_ref, ob_ref, oc_ref, od_ref)):
        gate = _sigmoid(_dot(g_low, wg_ref[n]) + bg_ref[n:n + 1, :])
        term = gate * jnp.dot(br_ref[...], wb_ref[n], preferred_element_type=F32)
        acc = term if acc is None else acc + term
    o_ref[...] = acc.astype(o_ref.dtype)


def _merge(proj, branches, w_gate, layer, b_gate, w_branch, col, tm=1024, tn=512):
    m = proj.shape[0]
    d_model = w_gate.shape[-1]
    tm = min(tm, m)
    br_spec = pl.BlockSpec((tm, BRANCH_WIDTH), lambda i, j: (i, 0))
    return pl.pallas_call(
        _merge_kernel,
        out_shape=jax.ShapeDtypeStruct((m, d_model), BF16),
        grid=(m // tm, d_model // tn),
        in_specs=[
            pl.BlockSpec((tm, GATE_RANK), lambda i, j: (i, col["glow"] // GATE_RANK)),
            br_spec, br_spec, br_spec, br_spec,
            pl.BlockSpec((None, N_BRANCH, GATE_RANK, tn), lambda i, j: (layer, 0, 0, j)),
            pl.BlockSpec((N_BRANCH, tn), lambda i, j: (0, j)),
            pl.BlockSpec((None, N_BRANCH, BRANCH_WIDTH, tn), lambda i, j: (layer, 0, 0, j)),
        ],
        out_specs=pl.BlockSpec((tm, tn), lambda i, j: (i, j)),
        compiler_params=_params(("parallel", "arbitrary"), VMEM_LIMIT),
        name="merge",
    )(proj, *branches, w_gate, b_gate, w_branch)


def _in_proj_layout(w_in):
    depth, d_model, _ = w_in.shape
    a_end = 4 * BRANCH_WIDTH
    small = 2 * DN_HEADS
    main = 6 * BRANCH_WIDTH + GATE_RANK
    pad = (-(a_end + main + small)) % 512
    w = jnp.concatenate([w_in[:, :, :a_end].astype(BF16), w_in[:, :, a_end + small:].astype(BF16),
                         w_in[:, :, a_end:a_end + small].astype(BF16),
                         jnp.zeros((depth, d_model, pad), BF16)], axis=2)
    w1 = BRANCH_WIDTH
    col = {"qkv": 0, "dz": 3 * w1, "lx": 4 * w1, "lz": 5 * w1, "su": 6 * w1, "sz": 7 * w1,
           "mq": 8 * w1, "mz": 9 * w1, "glow": 10 * w1, "small": 10 * w1 + GATE_RANK}
    return w, col


def kernel(x, mem, norm_w, w_in, dn_conv_w, dn_a_log, dn_dt_bias, dn_norm_w, lru_conv_w, lru_conv_b, lru_w_r, lru_b_r, lru_w_i, lru_b_i, lru_lambda, ssm_log_dt, ssm_a_re, ssm_a_im, ssm_b_re, ssm_b_im, ssm_c_re, ssm_c_im, ssm_d, ssm_w_glu, ssm_b_glu, mem_norm_w, w_kv, w_gate, b_gate, w_branch, w_out, final_norm_w):
    batch, seq, d_model = x.shape
    mem_len = mem.shape[1]
    depth = w_in.shape[0]
    xf = x.reshape(batch * seq, d_model)
    memf = mem.reshape(batch * mem_len, d_model)
    w_cat, col = _in_proj_layout(w_in)
    w_kv_b = w_kv.astype(BF16)
    w_gate_b = w_gate.astype(BF16)
    w_branch_b = w_branch.astype(BF16)
    w_out_b = w_out.astype(BF16)
    w_glu_b = ssm_w_glu.astype(BF16)
    w_ri = jnp.concatenate([lru_w_r, lru_w_i], axis=-1).astype(BF16)
    h = _rmsnorm(xf, norm_w[0], BF16)
    for l in range(depth):
        proj = _matmul(h, w_cat, l, F32)
        o_a = _deltanet(proj, dn_conv_w[l], dn_a_log[l], dn_dt_bias[l], dn_norm_w[l], batch, seq, col)
        o_b = _rglru(proj, lru_conv_w[l], lru_conv_b[l], w_ri, l, lru_b_r[l], lru_b_i[l],
                     lru_lambda[l], batch, seq, col)
        o_c = _s5(proj, ssm_log_dt[l], ssm_a_re[l], ssm_a_im[l], ssm_b_re[l], ssm_b_im[l],
                  ssm_c_re[l], ssm_c_im[l], ssm_d[l].reshape(-1), w_glu_b, l, ssm_b_glu[l], batch, seq, col)
        m_n = _rmsnorm(memf, mem_norm_w[l], BF16)
        kv = _matmul(m_n, w_kv_b, l, BF16)
        o_d = _mem_attn(proj, kv, batch, seq, mem_len, col)
        merged = _merge(proj, (o_a, o_b, o_c, o_d), w_gate_b, l, b_gate[l], w_branch_b, col)
        last = l == depth - 1
        next_norm_w = final_norm_w if last else norm_w[l + 1]
        xf, h = _out_proj_norm(merged, w_out_b, l, xf, next_norm_w, write_x=not last,
                               h_dtype=F32 if last else BF16)
    return h.reshape(batch, seq, d_model)
```

```python
import functools
import math

import jax
import jax.numpy as jnp
from jax import lax
from jax.experimental import pallas as pl
from jax.experimental.pallas import tpu as pltpu

F32 = jnp.float32
BF16 = jnp.bfloat16
HIGHEST = lax.Precision.HIGHEST

NORM_EPS = 1e-6
CONV_WIDTH = 4
HALO_ROWS = 8
N_BRANCH = 4
BRANCH_WIDTH = 1024
DN_HEAD_DIM = 128
DN_HEADS = 8
DN_CHUNK = 64
DN_TILE = 128
LRU_BLOCKS = 8
LRU_BLOCK = 128
LRU_C = 8.0
SSM_GROUP = 16
SSM_GROUPS = 64
SSM_STATE = 64
SSM_LANE_BLOCKS = 8
SSM_BLOCK_STATES = 512
MEM_HEADS = 4
MEM_HEAD_DIM = 256
GATE_RANK = 256
SMALL_COLS = 128

VMEM_LIMIT = 56 * 1024 * 1024


def _params(sem, vmem=None):
    return pltpu.CompilerParams(dimension_semantics=sem, vmem_limit_bytes=vmem)


def _sigmoid(x):
    return 0.5 * jnp.tanh(0.5 * x) + 0.5


def _silu(x):
    half = 0.5 * x
    return half * jnp.tanh(half) + half


def _softplus(x):
    return jnp.maximum(x, 0.0) + jnp.log1p(jnp.exp(-jnp.abs(x)))


def _dot(a, b):
    return jnp.dot(a.astype(BF16), b.astype(BF16), preferred_element_type=F32)


def _dot_nt(a, b):
    return lax.dot_general(a.astype(BF16), b.astype(BF16), (((1,), (1,)), ((), ())),
                           preferred_element_type=F32)


def _dot_f32(a, b):
    return jnp.dot(a, b, precision=HIGHEST, preferred_element_type=F32)


def _rmsnorm_kernel(x_ref, w_ref, o_ref):
    x = x_ref[...]
    var = jnp.mean(x * x, axis=-1, keepdims=True)
    o_ref[...] = (x * lax.rsqrt(var + NORM_EPS) * w_ref[...]).astype(o_ref.dtype)


def _rmsnorm(x, w, out_dtype, tm=256):
    m, d = x.shape
    tm = min(tm, m)
    return pl.pallas_call(
        _rmsnorm_kernel,
        out_shape=jax.ShapeDtypeStruct((m, d), out_dtype),
        grid=(m // tm,),
        in_specs=[pl.BlockSpec((tm, d), lambda i: (i, 0)),
                  pl.BlockSpec((1, d), lambda i: (0, 0))],
        out_specs=pl.BlockSpec((tm, d), lambda i: (i, 0)),
        compiler_params=_params(("parallel",)),
        name="rmsnorm",
    )(x, w.reshape(1, d))


def _mm_kernel(a_ref, w_ref, o_ref):
    o_ref[...] = jnp.dot(a_ref[...], w_ref[...], preferred_element_type=F32).astype(o_ref.dtype)


def _matmul(a, w, layer, out_dtype, tm=1024, tn=512):
    m, k = a.shape
    n = w.shape[2]
    tm = min(tm, m)
    tn = min(tn, n)
    return pl.pallas_call(
        _mm_kernel,
        out_shape=jax.ShapeDtypeStruct((m, n), out_dtype),
        grid=(m // tm, n // tn),
        in_specs=[pl.BlockSpec((tm, k), lambda i, j: (i, 0)),
                  pl.BlockSpec((None, k, tn), lambda i, j: (layer, 0, j))],
        out_specs=pl.BlockSpec((tm, tn), lambda i, j: (i, j)),
        compiler_params=_params(("parallel", "arbitrary"), VMEM_LIMIT),
        name="matmul",
    )(a, w)


def _out_norm_kernel(a_ref, w_ref, r_ref, nw_ref, *refs, write_x):
    if write_x:
        x_ref, h_ref, acc = refs
    else:
        h_ref, acc = refs
    j = pl.program_id(1)
    nj, _, tn = acc.shape
    acc[j] = r_ref[...] + jnp.dot(a_ref[...], w_ref[...], preferred_element_type=F32)

    @pl.when(j == nj - 1)
    def _():
        ssq = None
        for jj in range(nj):
            blk = acc[jj]
            part = jnp.sum(blk * blk, axis=-1, keepdims=True)
            ssq = part if ssq is None else ssq + part
        inv = lax.rsqrt(ssq * (1.0 / (nj * tn)) + NORM_EPS)
        for jj in range(nj):
            blk = acc[jj]
            cols = slice(jj * tn, (jj + 1) * tn)
            if write_x:
                x_ref[:, cols] = blk
            h_ref[:, cols] = (blk * inv * nw_ref[:, cols]).astype(h_ref.dtype)


def _out_proj_norm(a, w, layer, residual, norm_w, write_x, h_dtype, tm=512, tn=512):
    m, k = a.shape
    n = w.shape[2]
    tm = min(tm, m)
    row_spec = pl.BlockSpec((tm, n), lambda i, j: (i, 0))
    out_shape = [jax.ShapeDtypeStruct((m, n), h_dtype)]
    out_specs = [row_spec]
    if write_x:
        out_shape.insert(0, jax.ShapeDtypeStruct((m, n), F32))
        out_specs.insert(0, row_spec)
    outs = pl.pallas_call(
        functools.partial(_out_norm_kernel, write_x=write_x),
        out_shape=out_shape,
        grid=(m // tm, n // tn),
        in_specs=[pl.BlockSpec((tm, k), lambda i, j: (i, 0)),
                  pl.BlockSpec((None, k, tn), lambda i, j: (layer, 0, j)),
                  pl.BlockSpec((tm, tn), lambda i, j: (i, j)),
                  pl.BlockSpec((1, n), lambda i, j: (0, 0))],
        out_specs=out_specs,
        scratch_shapes=[pltpu.VMEM((n // tn, tm, tn), F32)],
        compiler_params=_params(("parallel", "arbitrary"), VMEM_LIMIT),
        name="out_proj_norm",
    )(a, w, residual, norm_w.reshape(1, n))
    return (outs[0], outs[1]) if write_x else (None, outs[0])


def _fill_conv_buffer(buf, halo_ref, x_ref, t, tm):
    @pl.when(t == 0)
    def _():
        buf[0:HALO_ROWS, :] = jnp.zeros((HALO_ROWS, buf.shape[1]), F32)

    @pl.when(t != 0)
    def _():
        buf[0:HALO_ROWS, :] = halo_ref[...]

    buf[HALO_ROWS:HALO_ROWS + tm, :] = x_ref[...]


def _causal_conv(buf, cw_ref, tm):
    assert CONV_WIDTH == 4
    x = buf[...]
    x1 = pltpu.roll(x, 1, 0)
    early = cw_ref[0:1, :] * x1 + cw_ref[1:2, :] * x
    late = cw_ref[2:3, :] * x1 + cw_ref[3:4, :] * x
    return (pltpu.roll(early, 2, 0) + late)[HALO_ROWS:HALO_ROWS + tm, :]


def _halo_index(seq_blocks, tile_blocks):
    def index(b, t, col):
        return (jnp.maximum(b * seq_blocks + t * tile_blocks - 1, 0), col)
    return index


def _dn_kernel(qkv_ref, halo_ref, z_ref, sm_ref, cw_ref, alog_ref, dtb_ref, nw_ref, o_ref,
               buf, state):
    t = pl.program_id(1)
    slots, tm, _ = qkv_ref.shape
    d = DN_HEAD_DIM

    @pl.when(t == 0)
    def _():
        state[...] = jnp.zeros(state.shape, F32)

    for slot in range(slots):
        _fill_conv_buffer(buf.at[slot], halo_ref.at[slot], qkv_ref.at[slot], t, tm)

    row = lax.broadcasted_iota(jnp.int32, (tm, tm), 0)
    col = lax.broadcasted_iota(jnp.int32, (tm, tm), 1)
    same_chunk = (row // DN_CHUNK) == (col // DN_CHUNK)
    causal = same_chunk & (row >= col)
    strict = same_chunk & (row > col)
    causal_f = causal.astype(F32)
    same_chunk_f = same_chunk.astype(F32)

    heads = range(DN_HEADS)
    chunks = range(tm // DN_CHUNK)
    zero_chunk = jnp.zeros((DN_CHUNK, 2 * d), F32)
    nw = nw_ref[...]

    def prepare(slot):
        qkv = _silu(_causal_conv(buf.at[slot], cw_ref, tm))
        small = sm_ref[slot]
        g_all = -jnp.exp(alog_ref[...]) * _softplus(small + dtb_ref[...])
        gc_all = _dot_f32(causal_f, g_all)
        return dict(qkv=qkv, beta=_sigmoid(small), gc=gc_all,
                    gl=_dot_f32(same_chunk_f, g_all),
                    gc_rows=gc_all.T)

    def head_inputs(ctx, h):
        qkv = ctx["qkv"]
        qh = qkv[:, h * d:(h + 1) * d]
        kh = qkv[:, BRANCH_WIDTH + h * d:BRANCH_WIDTH + (h + 1) * d]
        vh = qkv[:, 2 * BRANCH_WIDTH + h * d:2 * BRANCH_WIDTH + (h + 1) * d]
        qh = qh * (lax.rsqrt(jnp.sum(qh * qh, axis=-1, keepdims=True) + NORM_EPS) * (d ** -0.5))
        kh = kh * lax.rsqrt(jnp.sum(kh * kh, axis=-1, keepdims=True) + NORM_EPS)
        beta = ctx["beta"][:, h:h + 1]
        gc = ctx["gc"][:, DN_HEADS + h:DN_HEADS + h + 1]
        gl = ctx["gl"][:, DN_HEADS + h:DN_HEADS + h + 1]
        gr = ctx["gc_rows"][DN_HEADS + h:DN_HEADS + h + 1, :]
        eg = jnp.exp(gc)
        kbh = kh * beta
        return dict(
            q=qh.astype(BF16), k=kh.astype(BF16), kb=kbh.astype(BF16), qd=qh * eg,
            kd_t=(kh * jnp.exp(gl - gc)).T.astype(BF16),
            decay=jnp.where(causal, jnp.exp(jnp.where(causal, gc - gr, 0.0)), 0.0),
            sol=jnp.concatenate([vh * beta, kbh * eg], axis=-1),
            gl=gl)

    def solve_and_scan(slot, hd, between):
        x, qk = [], []
        for h in heads:
            both = lax.dot_general(jnp.concatenate([hd[h]["kb"], hd[h]["q"]], axis=0), hd[h]["k"],
                                   (((1,), (1,)), ((), ())), preferred_element_type=F32)
            x.append(jnp.where(strict, -(both[:tm] * hd[h]["decay"]), 0.0))
            qk.append(jnp.where(causal, both[tm:] * hd[h]["decay"], 0.0).astype(BF16))
        between()

        sol = [hd[h]["sol"] for h in heads]
        n_square = 5
        for it in range(n_square + 1):
            for h in heads:
                xb = x[h].astype(BF16)
                sb = sol[h].astype(BF16)
                if it < n_square:
                    y = jnp.dot(xb, jnp.concatenate([xb, sb], axis=-1), preferred_element_type=F32)
                    x[h] = y[:, :tm]
                    sol[h] = sol[h] + y[:, tm:]
                else:
                    sol[h] = sol[h] + jnp.dot(xb, sb, preferred_element_type=F32)
            between()

        kw, ku = [], []
        for h in heads:
            kw_h, ku_h = [], []
            for c in chunks:
                rows = slice(c * DN_CHUNK, (c + 1) * DN_CHUNK)
                wu = jnp.concatenate([sol[h][rows, d:], sol[h][rows, :d]], axis=-1)
                padded = jnp.concatenate([wu if i == c else zero_chunk for i in chunks], axis=0)
                prod = jnp.dot(hd[h]["kd_t"], padded.astype(BF16), preferred_element_type=F32)
                kw_h.append(prod[:, :d].astype(BF16))
                ku_h.append(prod[:, d:])
            kw.append(kw_h)
            ku.append(ku_h)
        between()

        s = [state[slot, h] for h in heads]
        inter = [[] for _ in heads]
        v_new = [[] for _ in heads]
        for c in chunks:
            rows = slice(c * DN_CHUNK, (c + 1) * DN_CHUNK)
            for h in heads:
                sb = s[h].astype(BF16)
                lhs = jnp.concatenate([hd[h]["qd"][rows], sol[h][rows, d:]], axis=0).astype(BF16)
                r = jnp.dot(lhs, sb, preferred_element_type=F32)
                inter[h].append(r[:DN_CHUNK])
                v_new[h].append(sol[h][rows, :d] - r[DN_CHUNK:])
                s[h] = (s[h] * jnp.exp(hd[h]["gl"][c * DN_CHUNK:c * DN_CHUNK + 1, :])
                        - jnp.dot(kw[h][c], sb, preferred_element_type=F32) + ku[h][c])
            between()
        z = z_ref[slot]
        for h in heads:
            state[slot, h] = s[h]
            out = jnp.concatenate(inter[h], axis=0) + jnp.dot(
                qk[h], jnp.concatenate(v_new[h], axis=0).astype(BF16), preferred_element_type=F32)
            var = jnp.mean(out * out, axis=-1, keepdims=True)
            out = out * lax.rsqrt(var + NORM_EPS) * nw
            o_ref[slot, :, h * d:(h + 1) * d] = (out * _silu(z[:, h * d:(h + 1) * d])).astype(o_ref.dtype)

    ctx = prepare(0)
    inputs = [head_inputs(ctx, h) for h in heads]
    for slot in range(slots):
        following = []
        pieces = []
        if slot + 1 < slots:
            pieces = [lambda nxt=slot + 1: following.append(prepare(nxt))] + [
                (lambda h=h: following.append(head_inputs(following[0], h))) for h in heads]

        def between(pieces=pieces):
            if pieces:
                pieces.pop(0)()

        solve_and_scan(slot, inputs, between)
        while pieces:
            pieces.pop(0)()
        inputs = following[1:]


def _deltanet(proj, conv_w, a_log, dt_bias, norm_w, batch, seq, col):
    tm = DN_TILE
    width = 3 * BRANCH_WIDTH
    slots = 2 if batch % 2 == 0 else 1
    pad = (0, SMALL_COLS - 2 * DN_HEADS)
    alog = jnp.pad(jnp.concatenate([jnp.zeros((DN_HEADS,), F32), a_log]), pad).reshape(1, SMALL_COLS)
    dtb = jnp.pad(jnp.concatenate([jnp.zeros((DN_HEADS,), F32), dt_bias]), pad).reshape(1, SMALL_COLS)
    nt = seq // tm
    proj3 = proj.reshape(batch, seq, proj.shape[1])
    halo_blocks = tm // HALO_ROWS
    out = pl.pallas_call(
        _dn_kernel,
        out_shape=jax.ShapeDtypeStruct((batch, seq, BRANCH_WIDTH), BF16),
        grid=(batch // slots, nt),
        in_specs=[
            pl.BlockSpec((slots, tm, width), lambda b, t: (b, t, col["qkv"] // width)),
            pl.BlockSpec((slots, HALO_ROWS, width),
                         lambda b, t: (b, jnp.maximum(t * halo_blocks - 1, 0), col["qkv"] // width)),
            pl.BlockSpec((slots, tm, BRANCH_WIDTH), lambda b, t: (b, t, col["dz"] // BRANCH_WIDTH)),
            pl.BlockSpec((slots, tm, SMALL_COLS), lambda b, t: (b, t, col["small"] // SMALL_COLS)),
            pl.BlockSpec((CONV_WIDTH, width), lambda b, t: (0, 0)),
            pl.BlockSpec((1, SMALL_COLS), lambda b, t: (0, 0)),
            pl.BlockSpec((1, SMALL_COLS), lambda b, t: (0, 0)),
            pl.BlockSpec((1, DN_HEAD_DIM), lambda b, t: (0, 0)),
        ],
        out_specs=pl.BlockSpec((slots, tm, BRANCH_WIDTH), lambda b, t: (b, t, 0)),
        scratch_shapes=[pltpu.VMEM((slots, HALO_ROWS + tm, width), F32),
                        pltpu.VMEM((slots, DN_HEADS, DN_HEAD_DIM, DN_HEAD_DIM), F32)],
        compiler_params=_params(("parallel", "arbitrary"), VMEM_LIMIT),
        name="deltanet",
    )(proj3, proj3, proj3, proj3, conv_w, alog, dtb, norm_w.reshape(1, DN_HEAD_DIM))
    return out.reshape(batch * seq, BRANCH_WIDTH)


def _lru_kernel(x_ref, halo_ref, z_ref, cw_ref, cb_ref, w_ref, b_ref, lam_ref, o_ref,
                buf, a_s, b_s, carry):
    t = pl.program_id(1)
    tm = x_ref.shape[0]
    width = x_ref.shape[1]

    @pl.when(t == 0)
    def _():
        carry[...] = jnp.zeros(carry.shape, F32)

    _fill_conv_buffer(buf, halo_ref, x_ref, t, tm)
    xc = _causal_conv(buf, cw_ref, tm) + cb_ref[...]
    gates = []
    for n in range(LRU_BLOCKS):
        gates.append(_dot(xc[:, n * LRU_BLOCK:(n + 1) * LRU_BLOCK], w_ref[n]))
    r = jnp.concatenate([g[:, :LRU_BLOCK] for g in gates], axis=-1)
    i = jnp.concatenate([g[:, LRU_BLOCK:] for g in gates], axis=-1)
    r = _sigmoid(r + b_ref[0:1, :])
    i = _sigmoid(i + b_ref[1:2, :])
    log_a = -LRU_C * r * _softplus(-lam_ref[...])
    a_s[...] = jnp.exp(log_a)
    one_minus_a2 = 1.0 - jnp.exp(2.0 * log_a)
    root = jnp.where(one_minus_a2 > 0.0, one_minus_a2 * lax.rsqrt(one_minus_a2), 0.0)
    b_s[...] = root * (i * xc)

    sub = lax.broadcasted_iota(jnp.int32, (HALO_ROWS, width), 0)

    def group(g, h_prev):
        rows = pl.ds(pl.multiple_of(g * HALO_ROWS, HALO_ROWS), HALO_ROWS)
        a = a_s[rows, :]
        b = b_s[rows, :]
        for shift in (1, 2, 4):
            keep = sub >= shift
            a_prev = jnp.where(keep, pltpu.roll(a, shift, 0), 1.0)
            b_prev = jnp.where(keep, pltpu.roll(b, shift, 0), 0.0)
            b = a * b_prev + b
            a = a * a_prev
        h = a * h_prev + b
        b_s[rows, :] = h
        return jnp.broadcast_to(h[HALO_ROWS - 1:HALO_ROWS, :], (HALO_ROWS, width))

    carry[...] = lax.fori_loop(0, tm // HALO_ROWS, group, carry[...])
    o_ref[...] = (b_s[...] * _silu(z_ref[...])).astype(o_ref.dtype)


def _rglru(proj, conv_w, conv_b, w_ri, layer, b_r, b_i, lam, batch, seq, col, tm=256):
    tm = min(tm, seq)
    width = BRANCH_WIDTH
    nt = seq // tm
    b_ri = jnp.stack([b_r, b_i])
    return pl.pallas_call(
        _lru_kernel,
        out_shape=jax.ShapeDtypeStruct((batch * seq, width), BF16),
        grid=(batch, nt),
        in_specs=[
            pl.BlockSpec((tm, width), lambda b, t: (b * nt + t, col["lx"] // width)),
            pl.BlockSpec((HALO_ROWS, width),
                         functools.partial(_halo_index(seq // HALO_ROWS, tm // HALO_ROWS),
                                           col=col["lx"] // width)),
            pl.BlockSpec((tm, width), lambda b, t: (b * nt + t, col["lz"] // width)),
            pl.BlockSpec((CONV_WIDTH, width), lambda b, t: (0, 0)),
            pl.BlockSpec((1, width), lambda b, t: (0, 0)),
            pl.BlockSpec((None, LRU_BLOCKS, LRU_BLOCK, 2 * LRU_BLOCK), lambda b, t: (layer, 0, 0, 0)),
            pl.BlockSpec((2, width), lambda b, t: (0, 0)),
            pl.BlockSpec((1, width), lambda b, t: (0, 0)),
        ],
        out_specs=pl.BlockSpec((tm, width), lambda b, t: (b * nt + t, 0)),
        scratch_shapes=[pltpu.VMEM((HALO_ROWS + tm, width), F32),
                        pltpu.VMEM((tm, width), F32),
                        pltpu.VMEM((tm, width), F32),
                        pltpu.VMEM((HALO_ROWS, width), F32)],
        compiler_params=_params(("parallel", "arbitrary"), VMEM_LIMIT),
        name="rglru",
    )(proj, proj, proj, conv_w, conv_b.reshape(1, width), w_ri, b_ri, lam.reshape(1, width))


def _zoh(ldt, a_re, a_im):
    dt = jnp.exp(ldt)
    mag = jnp.exp(dt * a_re)
    return mag * jnp.cos(dt * a_im), mag * jnp.sin(dt * a_im)


def _s5_discretize_kernel(ldt_ref, are_ref, aim_ref, bre_ref, bim_ref, ldtg_ref, areg_ref, aimg_ref,
                          bbre_ref, bbim_ref, pre_ref, pim_ref):
    a_re = are_ref[...]
    a_im = aim_ref[...]
    ab_re, ab_im = _zoh(ldt_ref[...], a_re, a_im)
    den = a_re * a_re + a_im * a_im
    f_re = ((ab_re - 1.0) * a_re + ab_im * a_im) / den
    f_im = (ab_im * a_re - (ab_re - 1.0) * a_im) / den
    b_re = bre_ref[...]
    b_im = bim_ref[...]
    bbre_ref[...] = f_re * b_re - f_im * b_im
    bbim_ref[...] = f_re * b_im + f_im * b_re

    seg_len = pre_ref.shape[0] - 2
    ab_re, ab_im = _zoh(ldtg_ref[...], areg_ref[...], aimg_ref[...])
    p_re, p_im = ab_re, ab_im
    pre_ref[0] = p_re
    pim_ref[0] = p_im
    for p in range(1, seg_len):
        p_re, p_im = p_re * ab_re - p_im * ab_im, p_re * ab_im + p_im * ab_re
        pre_ref[p] = p_re
        pim_ref[p] = p_im
    for p in range(seg_len, seg_len + 2):
        p_re, p_im = p_re * p_re - p_im * p_im, 2.0 * (p_re * p_im)
        pre_ref[p] = p_re
        pim_ref[p] = p_im


def _s5_discretize(log_dt, a_re, a_im, b_re, b_im, seg_len):
    rows = SSM_GROUPS * SSM_GROUP
    rep = lambda p: jnp.repeat(p, SSM_GROUP, axis=0)
    ldt = jnp.broadcast_to(log_dt[:, None], (SSM_GROUPS, SSM_STATE))
    to_rows = lambda b: b.transpose(0, 2, 1).reshape(rows, SSM_STATE)
    shp = jax.ShapeDtypeStruct((rows, SSM_STATE), F32)
    pshp = jax.ShapeDtypeStruct((seg_len + 2, SSM_GROUPS, SSM_STATE), F32)
    return pl.pallas_call(
        _s5_discretize_kernel, out_shape=(shp, shp, pshp, pshp), name="s5_discretize",
    )(rep(ldt), rep(a_re), rep(a_im), to_rows(b_re), to_rows(b_im), ldt, a_re, a_im)


def _s5_kernel(u_ref, z_ref, bw_ref, cw_ref, d_ref, a1_ref, hs_ref, al_ref,
               wg_ref, bg_ref, o_ref, up, xs, y_s, carry):
    t = pl.program_id(1)
    tm = u_ref.shape[0]
    seg = tm // HALO_ROWS
    ns = SSM_BLOCK_STATES
    ch = SSM_GROUP * (SSM_GROUPS // SSM_LANE_BLOCKS)

    @pl.when(t == 0)
    def _():
        carry[...] = jnp.zeros(carry.shape, F32)

    wrow = lax.broadcasted_iota(jnp.int32, (tm, tm), 0)
    trow = lax.broadcasted_iota(jnp.int32, (tm, tm), 1)
    select = (trow == (wrow % HALO_ROWS) * seg + wrow // HALO_ROWS).astype(BF16)
    up[...] = jnp.dot(select, u_ref[...].astype(BF16), preferred_element_type=F32).astype(BF16)

    sub = lax.broadcasted_iota(jnp.int32, (HALO_ROWS, ns), 0)
    zero = jnp.zeros((HALO_ROWS, ns), F32)
    def input_states(cb):
        xs[cb % 2] = jnp.dot(up[:, cb * ch:(cb + 1) * ch], bw_ref[cb], preferred_element_type=F32)

    input_states(0)
    for cb in range(SSM_LANE_BLOCKS):
        if cb + 1 < SSM_LANE_BLOCKS:
            input_states(cb + 1)
        xb = xs.at[cb % 2]
        ar = a1_ref[cb, :, 0:ns]
        ai = a1_ref[cb, :, ns:2 * ns]
        er, ei = zero, zero
        for i in range(seg):
            rows = slice(i * HALO_ROWS, (i + 1) * HALO_ROWS)
            er, ei = (xb[rows, 0:ns] + (ar * er - ai * ei),
                      xb[rows, ns:2 * ns] + (ar * ei + ai * er))

        fr = jnp.where(sub == 0, carry[cb, :, 0:ns], pltpu.roll(er, 1, 0))
        fi = jnp.where(sub == 0, carry[cb, :, ns:2 * ns], pltpu.roll(ei, 1, 0))
        for k, shift in enumerate((1, 2, 4)):
            mr = hs_ref[k, cb, :, 0:ns]
            mi = hs_ref[k, cb, :, ns:2 * ns]
            pr = pltpu.roll(fr, shift, 0)
            pi = pltpu.roll(fi, shift, 0)
            fr, fi = fr + (mr * pr - mi * pi), fi + (mr * pi + mi * pr)
        lr = al_ref[cb, :, 0:ns]
        li = al_ref[cb, :, ns:2 * ns]
        last = slice(HALO_ROWS - 1, HALO_ROWS)
        carry[cb, :, 0:ns] = jnp.broadcast_to((lr * fr - li * fi + er)[last, :], (HALO_ROWS, ns))
        carry[cb, :, ns:2 * ns] = jnp.broadcast_to((lr * fi + li * fr + ei)[last, :], (HALO_ROWS, ns))

        for i in range(seg):
            rows = slice(i * HALO_ROWS, (i + 1) * HALO_ROWS)
            fr, fi = (xb[rows, 0:ns] + (ar * fr - ai * fi),
                      xb[rows, ns:2 * ns] + (ar * fi + ai * fr))
            xb[rows, 0:ns] = fr
            xb[rows, ns:2 * ns] = fi
        cx = _dot(xb[...], cw_ref[cb])
        for i in range(seg):
            y_s[cb, pl.ds(i, HALO_ROWS, stride=seg), :] = cx[i * HALO_ROWS:(i + 1) * HALO_ROWS, :]

    y = jnp.concatenate([y_s[cb] for cb in range(SSM_LANE_BLOCKS)], axis=-1) + d_ref[...] * u_ref[...]
    y = 0.5 * y * (1.0 + jnp.tanh(math.sqrt(2.0 / math.pi) * (y + 0.044715 * (y * y * y))))
    glu = _dot(y, wg_ref[...]) + bg_ref[...]
    width = y.shape[1]
    o_ref[...] = (glu[:, :width] * _sigmoid(glu[:, width:]) * _silu(z_ref[...])).astype(o_ref.dtype)


def _s5(proj, log_dt, a_re, a_im, b_re, b_im, c_re, c_im, d_skip, w_glu, layer, b_glu, batch, seq, col, tm=256):
    tm = min(tm, seq)
    width = BRANCH_WIDTH
    nt = seq // tm
    seg = tm // HALO_ROWS
    nb = SSM_LANE_BLOCKS
    gb = SSM_GROUPS // nb
    ns = SSM_BLOCK_STATES
    bb_re, bb_im, p_re, p_im = _s5_discretize(log_dt, a_re, a_im, b_re, b_im, seg)
    eye = jnp.eye(gb, dtype=F32)

    def in_block(bb):
        bb = bb.reshape(nb, gb, SSM_GROUP, SSM_STATE)
        return jnp.einsum("bgcn,gh->bgchn", bb, eye).reshape(nb, gb * SSM_GROUP, ns)

    def out_block(c):
        c = c.reshape(nb, gb, SSM_GROUP, SSM_STATE)
        return jnp.einsum("bgcn,gh->bgnhc", c, eye).reshape(nb, ns, gb * SSM_GROUP)

    bw = jnp.concatenate([in_block(bb_re), in_block(bb_im)], axis=-1).astype(BF16)
    cw = jnp.concatenate([out_block(c_re), -out_block(c_im)], axis=1).astype(BF16)

    def lanes(p):
        return p.reshape(p.shape[0], nb, ns).transpose(1, 0, 2)

    sub = jnp.arange(HALO_ROWS)[:, None, None]

    def rows8(p_re_1, p_im_1, first_row=0):
        pr = jnp.where(sub >= first_row, p_re_1[None], 0.0)
        pi = jnp.where(sub >= first_row, p_im_1[None], 0.0)
        return jnp.concatenate([lanes(pr), lanes(pi)], axis=-1)

    a1 = rows8(p_re[0], p_im[0])
    al = rows8(p_re[seg - 1], p_im[seg - 1])
    hs = jnp.stack([rows8(p_re[seg - 1], p_im[seg - 1], 1), rows8(p_re[seg], p_im[seg], 2),
                    rows8(p_re[seg + 1], p_im[seg + 1], 4)])
    mult_spec = pl.BlockSpec((nb, HALO_ROWS, 2 * ns), lambda b, t: (0, 0, 0))
    return pl.pallas_call(
        _s5_kernel,
        out_shape=jax.ShapeDtypeStruct((batch * seq, width), BF16),
        grid=(batch, nt),
        in_specs=[
            pl.BlockSpec((tm, width), lambda b, t: (b * nt + t, col["su"] // width)),
            pl.BlockSpec((tm, width), lambda b, t: (b * nt + t, col["sz"] // width)),
            pl.BlockSpec((nb, gb * SSM_GROUP, 2 * ns), lambda b, t: (0, 0, 0)),
            pl.BlockSpec((nb, 2 * ns, gb * SSM_GROUP), lambda b, t: (0, 0, 0)),
            pl.BlockSpec((1, width), lambda b, t: (0, 0)),
            mult_spec,
            pl.BlockSpec((3, nb, HALO_ROWS, 2 * ns), lambda b, t: (0, 0, 0, 0)),
            mult_spec,
            pl.BlockSpec((None, width, 2 * width), lambda b, t: (layer, 0, 0)),
            pl.BlockSpec((1, 2 * width), lambda b, t: (0, 0)),
        ],
        out_specs=pl.BlockSpec((tm, width), lambda b, t: (b * nt + t, 0)),
        scratch_shapes=[pltpu.VMEM((tm, width), BF16),
                        pltpu.VMEM((2, tm, 2 * ns), F32),
                        pltpu.VMEM((nb, tm, gb * SSM_GROUP), F32),
                        pltpu.VMEM((nb, HALO_ROWS, 2 * ns), F32)],
        compiler_params=_params(("parallel", "arbitrary"), VMEM_LIMIT),
        name="s5",
    )(proj, proj, bw, cw, d_skip.reshape(1, width), a1, hs, al, w_glu, b_glu.reshape(1, 2 * width))


def _mem_attn_kernel(q_ref, z_ref, k_ref, v_ref, o_ref):
    dh = MEM_HEAD_DIM
    z = z_ref[...]
    for h in range(MEM_HEADS):
        cols = slice(h * dh, (h + 1) * dh)
        s = _dot_nt(q_ref[:, cols], k_ref[:, cols]) * (dh ** -0.5)
        s = s - jnp.max(s, axis=-1, keepdims=True)
        p = jnp.exp(s)
        p = p / jnp.sum(p, axis=-1, keepdims=True)
        o = _dot(p, v_ref[:, cols])
        o_ref[:, cols] = (o * _silu(z[:, cols])).astype(o_ref.dtype)


def _mem_attn(proj, kv, batch, seq, mem_len, col, tm=256):
    tm = min(tm, seq)
    width = BRANCH_WIDTH
    nt = seq // tm
    return pl.pallas_call(
        _mem_attn_kernel,
        out_shape=jax.ShapeDtypeStruct((batch * seq, width), BF16),
        grid=(batch, nt),
        in_specs=[
            pl.BlockSpec((tm, width), lambda b, t: (b * nt + t, col["mq"] // width)),
            pl.BlockSpec((tm, width), lambda b, t: (b * nt + t, col["mz"] // width)),
            pl.BlockSpec((mem_len, width), lambda b, t: (b, 0)),
            pl.BlockSpec((mem_len, width), lambda b, t: (b, 1)),
        ],
        out_specs=pl.BlockSpec((tm, width), lambda b, t: (b * nt + t, 0)),
        compiler_params=_params(("parallel", "parallel"), VMEM_LIMIT),
        name="mem_attn",
    )(proj, proj, kv, kv)


def _merge_kernel(g_ref, oa_ref, ob_ref, oc_ref, od_ref, wg_ref, bg_ref, wb_ref, o_ref):
    g_low = g_ref[...]
    acc = None
    for n, br_ref in enumerate((oa_ref, ob_ref, oc_ref, od_ref)):
        gate = _sigmoid(_dot(g_low, wg_ref[n]) + bg_ref[n:n + 1, :])
        term = gate * jnp.dot(br_ref[...], wb_ref[n], preferred_element_type=F32)
        acc = term if acc is None else acc + term
    o_ref[...] = acc.astype(o_ref.dtype)


def _merge(proj, branches, w_gate, layer, b_gate, w_branch, col, tm=1024, tn=512):
    m = proj.shape[0]
    d_model = w_gate.shape[-1]
    tm = min(tm, m)
    br_spec = pl.BlockSpec((tm, BRANCH_WIDTH), lambda i, j: (i, 0))
    return pl.pallas_call(
        _merge_kernel,
        out_shape=jax.ShapeDtypeStruct((m, d_model), BF16),
        grid=(m // tm, d_model // tn),
        in_specs=[
            pl.BlockSpec((tm, GATE_RANK), lambda i, j: (i, col["glow"] // GATE_RANK)),
            br_spec, br_spec, br_spec, br_spec,
            pl.BlockSpec((None, N_BRANCH, GATE_RANK, tn), lambda i, j: (layer, 0, 0, j)),
            pl.BlockSpec((N_BRANCH, tn), lambda i, j: (0, j)),
            pl.BlockSpec((None, N_BRANCH, BRANCH_WIDTH, tn), lambda i, j: (layer, 0, 0, j)),
        ],
        out_specs=pl.BlockSpec((tm, tn), lambda i, j: (i, j)),
        compiler_params=_params(("parallel", "arbitrary"), VMEM_LIMIT),
        name="merge",
    )(proj, *branches, w_gate, b_gate, w_branch)


RELAYOUT_COLS = 512


def _relayout_kernel(cur_ref, nxt_ref, small_ref, o_ref, prev, *, n_aligned, shift, tail):
    c = pl.program_id(2)
    last = pl.num_programs(2) - 1
    width = o_ref.shape[1]
    lane = lax.broadcasted_iota(jnp.int32, o_ref.shape, 1)

    def shifted(left):
        return jnp.where(lane < width - shift, pltpu.roll(left, width - shift, 1),
                         pltpu.roll(nxt_ref[...], width - shift, 1))

    @pl.when(c < n_aligned)
    def _():
        o_ref[...] = cur_ref[...].astype(o_ref.dtype)

    @pl.when(c == n_aligned)
    def _():
        o_ref[...] = shifted(cur_ref[...]).astype(o_ref.dtype)

    @pl.when((c > n_aligned) & (c < last))
    def _():
        o_ref[...] = shifted(prev[...]).astype(o_ref.dtype)

    @pl.when(c == last)
    def _():
        rest = pltpu.roll(prev[...], width - shift, 1)
        skipped = pltpu.roll(small_ref[...], tail, 1)
        o_ref[...] = jnp.where(lane < tail, rest,
                               jnp.where(lane < tail + shift, skipped, 0.0)).astype(o_ref.dtype)

    @pl.when(c >= n_aligned)
    def _():
        prev[...] = nxt_ref[...]


def _in_proj_layout(w_in, tk=1024):
    depth, d_model, d_in = w_in.shape
    cb = RELAYOUT_COLS
    a_end = 4 * BRANCH_WIDTH
    small = 2 * DN_HEADS
    main = 6 * BRANCH_WIDTH + GATE_RANK
    assert d_in == a_end + small + main and a_end % cb == 0
    n_aligned = a_end // cb
    n_out = -(-d_in // cb)
    tail = main - (n_out - 1 - n_aligned) * cb
    assert 0 < tail and tail + small <= cb
    n_in = -(-d_in // cb)
    tk = min(tk, d_model)
    blk = lambda index: pl.BlockSpec((None, tk, cb), index)
    w = pl.pallas_call(
        functools.partial(_relayout_kernel, n_aligned=n_aligned, shift=small, tail=tail),
        out_shape=jax.ShapeDtypeStruct((depth, d_model, n_out * cb), BF16),
        grid=(depth, d_model // tk, n_out),
        in_specs=[blk(lambda l, k, c: (l, k, jnp.minimum(c, n_aligned))),
                  blk(lambda l, k, c: (l, k, jnp.clip(c + 1, n_aligned, n_in - 1))),
                  blk(lambda l, k, c: (l, k, n_aligned))],
        out_specs=blk(lambda l, k, c: (l, k, c)),
        scratch_shapes=[pltpu.VMEM((tk, cb), F32)],
        compiler_params=_params(("parallel", "parallel", "arbitrary")),
        name="in_proj_relayout",
    )(w_in, w_in, w_in)
    w1 = BRANCH_WIDTH
    col = {"qkv": 0, "dz": 3 * w1, "lx": 4 * w1, "lz": 5 * w1, "su": 6 * w1, "sz": 7 * w1,
           "mq": 8 * w1, "mz": 9 * w1, "glow": 10 * w1, "small": 10 * w1 + GATE_RANK}
    return w, col


def kernel(x, mem, norm_w, w_in, dn_conv_w, dn_a_log, dn_dt_bias, dn_norm_w, lru_conv_w, lru_conv_b, lru_w_r, lru_b_r, lru_w_i, lru_b_i, lru_lambda, ssm_log_dt, ssm_a_re, ssm_a_im, ssm_b_re, ssm_b_im, ssm_c_re, ssm_c_im, ssm_d, ssm_w_glu, ssm_b_glu, mem_norm_w, w_kv, w_gate, b_gate, w_branch, w_out, final_norm_w):
    batch, seq, d_model = x.shape
    mem_len = mem.shape[1]
    depth = w_in.shape[0]
    xf = x.reshape(batch * seq, d_model)
    memf = mem.reshape(batch * mem_len, d_model)
    w_cat, col = _in_proj_layout(w_in)
    w_kv_b = w_kv.astype(BF16)
    w_gate_b = w_gate.astype(BF16)
    w_branch_b = w_branch.astype(BF16)
    w_out_b = w_out.astype(BF16)
    w_glu_b = ssm_w_glu.astype(BF16)
    w_ri = jnp.concatenate([lru_w_r, lru_w_i], axis=-1).astype(BF16)
    h = _rmsnorm(xf, norm_w[0], BF16)
    for l in range(depth):
        proj = _matmul(h, w_cat, l, F32)
        o_a = _deltanet(proj, dn_conv_w[l], dn_a_log[l], dn_dt_bias[l], dn_norm_w[l], batch, seq, col)
        o_b = _rglru(proj, lru_conv_w[l], lru_conv_b[l], w_ri, l, lru_b_r[l], lru_b_i[l],
                     lru_lambda[l], batch, seq, col)
        o_c = _s5(proj, ssm_log_dt[l], ssm_a_re[l], ssm_a_im[l], ssm_b_re[l], ssm_b_im[l],
                  ssm_c_re[l], ssm_c_im[l], ssm_d[l].reshape(-1), w_glu_b, l, ssm_b_glu[l], batch, seq, col)
        m_n = _rmsnorm(memf, mem_norm_w[l], BF16)
        kv = _matmul(m_n, w_kv_b, l, BF16)
        o_d = _mem_attn(proj, kv, batch, seq, mem_len, col)
        merged = _merge(proj, (o_a, o_b, o_c, o_d), w_gate_b, l, b_gate[l], w_branch_b, col)
        last = l == depth - 1
        next_norm_w = final_norm_w if last else norm_w[l + 1]
        xf, h = _out_proj_norm(merged, w_out_b, l, xf, next_norm_w, write_x=not last,
                               h_dtype=F32 if last else BF16)
    return h.reshape(batch, seq, d_model)
```

```python
import functools
import math

import jax
import jax.numpy as jnp
from jax import lax
from jax.experimental import pallas as pl
from jax.experimental.pallas import tpu as pltpu

F32 = jnp.float32
BF16 = jnp.bfloat16
HIGHEST = lax.Precision.HIGHEST

NORM_EPS = 1e-6
CONV_WIDTH = 4
HALO_ROWS = 8
N_BRANCH = 4
BRANCH_WIDTH = 1024
DN_HEAD_DIM = 128
DN_HEADS = 8
DN_CHUNK = 64
DN_TILE = 128
LRU_BLOCKS = 8
LRU_BLOCK = 128
LRU_C = 8.0
SSM_GROUP = 16
SSM_GROUPS = 64
SSM_STATE = 64
SSM_LANE_BLOCKS = 8
SSM_BLOCK_STATES = 512
MEM_HEADS = 4
MEM_HEAD_DIM = 256
GATE_RANK = 256
SMALL_COLS = 128

VMEM_LIMIT = 56 * 1024 * 1024


def _params(sem, vmem=None):
    return pltpu.CompilerParams(dimension_semantics=sem, vmem_limit_bytes=vmem)


def _sigmoid(x):
    return 0.5 * jnp.tanh(0.5 * x) + 0.5


def _silu(x):
    half = 0.5 * x
    return half * jnp.tanh(half) + half


def _softplus(x):
    return jnp.maximum(x, 0.0) + jnp.log1p(jnp.exp(-jnp.abs(x)))


def _dot(a, b):
    return jnp.dot(a.astype(BF16), b.astype(BF16), preferred_element_type=F32)


def _dot_nt(a, b):
    return lax.dot_general(a.astype(BF16), b.astype(BF16), (((1,), (1,)), ((), ())),
                           preferred_element_type=F32)


def _dot_f32(a, b):
    return jnp.dot(a, b, precision=HIGHEST, preferred_element_type=F32)


def _rmsnorm_kernel(x_ref, w_ref, o_ref):
    x = x_ref[...]
    var = jnp.mean(x * x, axis=-1, keepdims=True)
    o_ref[...] = (x * lax.rsqrt(var + NORM_EPS) * w_ref[...]).astype(o_ref.dtype)


def _rmsnorm(x, w, out_dtype, tm=256):
    m, d = x.shape
    tm = min(tm, m)
    return pl.pallas_call(
        _rmsnorm_kernel,
        out_shape=jax.ShapeDtypeStruct((m, d), out_dtype),
        grid=(m // tm,),
        in_specs=[pl.BlockSpec((tm, d), lambda i: (i, 0)),
                  pl.BlockSpec((1, d), lambda i: (0, 0))],
        out_specs=pl.BlockSpec((tm, d), lambda i: (i, 0)),
        compiler_params=_params(("parallel",)),
        name="rmsnorm",
    )(x, w.reshape(1, d))


def _mm_kernel(a_ref, w_ref, o_ref):
    o_ref[...] = jnp.dot(a_ref[...], w_ref[...], preferred_element_type=F32).astype(o_ref.dtype)


def _matmul(a, w, layer, out_dtype, tm=1024, tn=512):
    m, k = a.shape
    n = w.shape[2]
    tm = min(tm, m)
    tn = min(tn, n)
    return pl.pallas_call(
        _mm_kernel,
        out_shape=jax.ShapeDtypeStruct((m, n), out_dtype),
        grid=(m // tm, n // tn),
        in_specs=[pl.BlockSpec((tm, k), lambda i, j: (i, 0)),
                  pl.BlockSpec((None, k, tn), lambda i, j: (layer, 0, j))],
        out_specs=pl.BlockSpec((tm, tn), lambda i, j: (i, j)),
        compiler_params=_params(("parallel", "arbitrary"), VMEM_LIMIT),
        name="matmul",
    )(a, w)


def _out_norm_kernel(a_ref, w_ref, r_ref, nw_ref, *refs, write_x):
    if write_x:
        x_ref, h_ref, acc = refs
    else:
        h_ref, acc = refs
    j = pl.program_id(1)
    nj, _, tn = acc.shape
    acc[j] = r_ref[...] + jnp.dot(a_ref[...], w_ref[...], preferred_element_type=F32)

    @pl.when(j == nj - 1)
    def _():
        ssq = None
        for jj in range(nj):
            blk = acc[jj]
            part = jnp.sum(blk * blk, axis=-1, keepdims=True)
            ssq = part if ssq is None else ssq + part
        inv = lax.rsqrt(ssq * (1.0 / (nj * tn)) + NORM_EPS)
        for jj in range(nj):
            blk = acc[jj]
            cols = slice(jj * tn, (jj + 1) * tn)
            if write_x:
                x_ref[:, cols] = blk
            h_ref[:, cols] = (blk * inv * nw_ref[:, cols]).astype(h_ref.dtype)


def _out_proj_norm(a, w, layer, residual, norm_w, write_x, h_dtype, tm=512, tn=512):
    m, k = a.shape
    n = w.shape[2]
    tm = min(tm, m)
    row_spec = pl.BlockSpec((tm, n), lambda i, j: (i, 0))
    out_shape = [jax.ShapeDtypeStruct((m, n), h_dtype)]
    out_specs = [row_spec]
    if write_x:
        out_shape.insert(0, jax.ShapeDtypeStruct((m, n), F32))
        out_specs.insert(0, row_spec)
    outs = pl.pallas_call(
        functools.partial(_out_norm_kernel, write_x=write_x),
        out_shape=out_shape,
        grid=(m // tm, n // tn),
        in_specs=[pl.BlockSpec((tm, k), lambda i, j: (i, 0)),
                  pl.BlockSpec((None, k, tn), lambda i, j: (layer, 0, j)),
                  pl.BlockSpec((tm, tn), lambda i, j: (i, j)),
                  pl.BlockSpec((1, n), lambda i, j: (0, 0))],
        out_specs=out_specs,
        scratch_shapes=[pltpu.VMEM((n // tn, tm, tn), F32)],
        compiler_params=_params(("parallel", "arbitrary"), VMEM_LIMIT),
        name="out_proj_norm",
    )(a, w, residual, norm_w.reshape(1, n))
    return (outs[0], outs[1]) if write_x else (None, outs[0])


def _fill_conv_buffer(buf, halo_ref, x_ref, t, tm):
    @pl.when(t == 0)
    def _():
        buf[0:HALO_ROWS, :] = jnp.zeros((HALO_ROWS, buf.shape[1]), F32)

    @pl.when(t != 0)
    def _():
        buf[0:HALO_ROWS, :] = halo_ref[...]

    buf[HALO_ROWS:HALO_ROWS + tm, :] = x_ref[...]


def _causal_conv(buf, cw_ref, tm):
    assert CONV_WIDTH == 4
    x = buf[...]
    x1 = pltpu.roll(x, 1, 0)
    early = cw_ref[0:1, :] * x1 + cw_ref[1:2, :] * x
    late = cw_ref[2:3, :] * x1 + cw_ref[3:4, :] * x
    return (pltpu.roll(early, 2, 0) + late)[HALO_ROWS:HALO_ROWS + tm, :]


def _halo_index(seq_blocks, tile_blocks):
    def index(b, t, col):
        return (jnp.maximum(b * seq_blocks + t * tile_blocks - 1, 0), col)
    return index


def _dn_kernel(qkv_ref, halo_ref, z_ref, sm_ref, cw_ref, alog_ref, dtb_ref, nw_ref, o_ref,
               buf, state):
    t = pl.program_id(1)
    slots, tm, _ = qkv_ref.shape
    d = DN_HEAD_DIM

    @pl.when(t == 0)
    def _():
        state[...] = jnp.zeros(state.shape, F32)

    for slot in range(slots):
        _fill_conv_buffer(buf.at[slot], halo_ref.at[slot], qkv_ref.at[slot], t, tm)

    row = lax.broadcasted_iota(jnp.int32, (tm, tm), 0)
    col = lax.broadcasted_iota(jnp.int32, (tm, tm), 1)
    same_chunk = (row // DN_CHUNK) == (col // DN_CHUNK)
    causal = same_chunk & (row >= col)
    strict = same_chunk & (row > col)
    causal_f = causal.astype(F32)
    same_chunk_f = same_chunk.astype(F32)

    heads = range(DN_HEADS)
    chunks = range(tm // DN_CHUNK)
    zero_chunk = jnp.zeros((DN_CHUNK, 2 * d), F32)
    nw = nw_ref[...]

    def prepare(slot):
        qkv = _silu(_causal_conv(buf.at[slot], cw_ref, tm))
        small = sm_ref[slot]
        g_all = -jnp.exp(alog_ref[...]) * _softplus(small + dtb_ref[...])
        gc_all = _dot_f32(causal_f, g_all)
        return dict(qkv=qkv, beta=_sigmoid(small), gc=gc_all,
                    gl=_dot_f32(same_chunk_f, g_all),
                    gc_rows=gc_all.T)

    def head_inputs(ctx, h):
        qkv = ctx["qkv"]
        qh = qkv[:, h * d:(h + 1) * d]
        kh = qkv[:, BRANCH_WIDTH + h * d:BRANCH_WIDTH + (h + 1) * d]
        vh = qkv[:, 2 * BRANCH_WIDTH + h * d:2 * BRANCH_WIDTH + (h + 1) * d]
        qh = qh * (lax.rsqrt(jnp.sum(qh * qh, axis=-1, keepdims=True) + NORM_EPS) * (d ** -0.5))
        kh = kh * lax.rsqrt(jnp.sum(kh * kh, axis=-1, keepdims=True) + NORM_EPS)
        beta = ctx["beta"][:, h:h + 1]
        gc = ctx["gc"][:, DN_HEADS + h:DN_HEADS + h + 1]
        gl = ctx["gl"][:, DN_HEADS + h:DN_HEADS + h + 1]
        gr = ctx["gc_rows"][DN_HEADS + h:DN_HEADS + h + 1, :]
        eg = jnp.exp(gc)
        kbh = kh * beta
        return dict(
            q=qh.astype(BF16), k=kh.astype(BF16), kb=kbh.astype(BF16), qd=qh * eg,
            kd_t=(kh * jnp.exp(gl - gc)).T.astype(BF16),
            decay=jnp.where(causal, jnp.exp(jnp.where(causal, gc - gr, 0.0)), 0.0),
            sol=jnp.concatenate([vh * beta, kbh * eg], axis=-1),
            gl=gl)

    def solve_and_scan(slot, hd, between):
        x, qk = [], []
        for h in heads:
            both = lax.dot_general(jnp.concatenate([hd[h]["kb"], hd[h]["q"]], axis=0), hd[h]["k"],
                                   (((1,), (1,)), ((), ())), preferred_element_type=F32)
            x.append(jnp.where(strict, -(both[:tm] * hd[h]["decay"]), 0.0))
            qk.append(jnp.where(causal, both[tm:] * hd[h]["decay"], 0.0).astype(BF16))
        between()

        sol = [hd[h]["sol"] for h in heads]
        n_square = 5
        for it in range(n_square + 1):
            for h in heads:
                xb = x[h].astype(BF16)
                sb = sol[h].astype(BF16)
                if it < n_square:
                    y = jnp.dot(xb, jnp.concatenate([xb, sb], axis=-1), preferred_element_type=F32)
                    x[h] = y[:, :tm]
                    sol[h] = sol[h] + y[:, tm:]
                else:
                    sol[h] = sol[h] + jnp.dot(xb, sb, preferred_element_type=F32)
            between()

        kw, ku = [], []
        for h in heads:
            kw_h, ku_h = [], []
            for c in chunks:
                rows = slice(c * DN_CHUNK, (c + 1) * DN_CHUNK)
                wu = jnp.concatenate([sol[h][rows, d:], sol[h][rows, :d]], axis=-1)
                padded = jnp.concatenate([wu if i == c else zero_chunk for i in chunks], axis=0)
                prod = jnp.dot(hd[h]["kd_t"], padded.astype(BF16), preferred_element_type=F32)
                kw_h.append(prod[:, :d].astype(BF16))
                ku_h.append(prod[:, d:])
            kw.append(kw_h)
            ku.append(ku_h)
        between()

        s = [state[slot, h] for h in heads]
        inter = [[] for _ in heads]
        v_new = [[] for _ in heads]
        for c in chunks:
            rows = slice(c * DN_CHUNK, (c + 1) * DN_CHUNK)
            for h in heads:
                sb = s[h].astype(BF16)
                lhs = jnp.concatenate([hd[h]["qd"][rows], sol[h][rows, d:]], axis=0).astype(BF16)
                r = jnp.dot(lhs, sb, preferred_element_type=F32)
                inter[h].append(r[:DN_CHUNK])
                v_new[h].append(sol[h][rows, :d] - r[DN_CHUNK:])
                s[h] = (s[h] * jnp.exp(hd[h]["gl"][c * DN_CHUNK:c * DN_CHUNK + 1, :])
                        - jnp.dot(kw[h][c], sb, preferred_element_type=F32) + ku[h][c])
            between()
        z = z_ref[slot]
        for h in heads:
            state[slot, h] = s[h]
            out = jnp.concatenate(inter[h], axis=0) + jnp.dot(
                qk[h], jnp.concatenate(v_new[h], axis=0).astype(BF16), preferred_element_type=F32)
            var = jnp.mean(out * out, axis=-1, keepdims=True)
            out = out * lax.rsqrt(var + NORM_EPS) * nw
            o_ref[slot, :, h * d:(h + 1) * d] = (out * _silu(z[:, h * d:(h + 1) * d])).astype(o_ref.dtype)

    ctx = prepare(0)
    inputs = [head_inputs(ctx, h) for h in heads]
    for slot in range(slots):
        following = []
        pieces = []
        if slot + 1 < slots:
            pieces = [lambda nxt=slot + 1: following.append(prepare(nxt))] + [
                (lambda h=h: following.append(head_inputs(following[0], h))) for h in heads]

        def between(pieces=pieces):
            if pieces:
                pieces.pop(0)()

        solve_and_scan(slot, inputs, between)
        while pieces:
            pieces.pop(0)()
        inputs = following[1:]


def _deltanet(proj, conv_w, a_log, dt_bias, norm_w, batch, seq, col):
    tm = DN_TILE
    width = 3 * BRANCH_WIDTH
    slots = 2 if batch % 2 == 0 else 1
    pad = (0, SMALL_COLS - 2 * DN_HEADS)
    alog = jnp.pad(jnp.concatenate([jnp.zeros((DN_HEADS,), F32), a_log]), pad).reshape(1, SMALL_COLS)
    dtb = jnp.pad(jnp.concatenate([jnp.zeros((DN_HEADS,), F32), dt_bias]), pad).reshape(1, SMALL_COLS)
    nt = seq // tm
    proj3 = proj.reshape(batch, seq, proj.shape[1])
    halo_blocks = tm // HALO_ROWS
    out = pl.pallas_call(
        _dn_kernel,
        out_shape=jax.ShapeDtypeStruct((batch, seq, BRANCH_WIDTH), BF16),
        grid=(batch // slots, nt),
        in_specs=[
            pl.BlockSpec((slots, tm, width), lambda b, t: (b, t, col["qkv"] // width)),
            pl.BlockSpec((slots, HALO_ROWS, width),
                         lambda b, t: (b, jnp.maximum(t * halo_blocks - 1, 0), col["qkv"] // width)),
            pl.BlockSpec((slots, tm, BRANCH_WIDTH), lambda b, t: (b, t, col["dz"] // BRANCH_WIDTH)),
            pl.BlockSpec((slots, tm, SMALL_COLS), lambda b, t: (b, t, col["small"] // SMALL_COLS)),
            pl.BlockSpec((CONV_WIDTH, width), lambda b, t: (0, 0)),
            pl.BlockSpec((1, SMALL_COLS), lambda b, t: (0, 0)),
            pl.BlockSpec((1, SMALL_COLS), lambda b, t: (0, 0)),
            pl.BlockSpec((1, DN_HEAD_DIM), lambda b, t: (0, 0)),
        ],
        out_specs=pl.BlockSpec((slots, tm, BRANCH_WIDTH), lambda b, t: (b, t, 0)),
        scratch_shapes=[pltpu.VMEM((slots, HALO_ROWS + tm, width), F32),
                        pltpu.VMEM((slots, DN_HEADS, DN_HEAD_DIM, DN_HEAD_DIM), F32)],
        compiler_params=_params(("parallel", "arbitrary"), VMEM_LIMIT),
        name="deltanet",
    )(proj3, proj3, proj3, proj3, conv_w, alog, dtb, norm_w.reshape(1, DN_HEAD_DIM))
    return out.reshape(batch * seq, BRANCH_WIDTH)


def _lru_kernel(x_ref, halo_ref, z_ref, cw_ref, cb_ref, w_ref, b_ref, lam_ref, o_ref,
                buf, a_s, b_s, carry):
    t = pl.program_id(1)
    tm = x_ref.shape[0]
    width = x_ref.shape[1]

    @pl.when(t == 0)
    def _():
        carry[...] = jnp.zeros(carry.shape, F32)

    _fill_conv_buffer(buf, halo_ref, x_ref, t, tm)
    xc = _causal_conv(buf, cw_ref, tm) + cb_ref[...]
    gates = []
    for n in range(LRU_BLOCKS):
        gates.append(_dot(xc[:, n * LRU_BLOCK:(n + 1) * LRU_BLOCK], w_ref[n]))
    r = jnp.concatenate([g[:, :LRU_BLOCK] for g in gates], axis=-1)
    i = jnp.concatenate([g[:, LRU_BLOCK:] for g in gates], axis=-1)
    r = _sigmoid(r + b_ref[0:1, :])
    i = _sigmoid(i + b_ref[1:2, :])
    log_a = -LRU_C * r * _softplus(-lam_ref[...])
    a_s[...] = jnp.exp(log_a)
    one_minus_a2 = 1.0 - jnp.exp(2.0 * log_a)
    root = jnp.where(one_minus_a2 > 0.0, one_minus_a2 * lax.rsqrt(one_minus_a2), 0.0)
    b_s[...] = root * (i * xc)

    sub = lax.broadcasted_iota(jnp.int32, (HALO_ROWS, width), 0)

    def group(g, h_prev):
        rows = pl.ds(pl.multiple_of(g * HALO_ROWS, HALO_ROWS), HALO_ROWS)
        a = a_s[rows, :]
        b = b_s[rows, :]
        for shift in (1, 2, 4):
            keep = sub >= shift
            a_prev = jnp.where(keep, pltpu.roll(a, shift, 0), 1.0)
            b_prev = jnp.where(keep, pltpu.roll(b, shift, 0), 0.0)
            b = a * b_prev + b
            a = a * a_prev
        h = a * h_prev + b
        b_s[rows, :] = h
        return jnp.broadcast_to(h[HALO_ROWS - 1:HALO_ROWS, :], (HALO_ROWS, width))

    carry[...] = lax.fori_loop(0, tm // HALO_ROWS, group, carry[...])
    o_ref[...] = (b_s[...] * _silu(z_ref[...])).astype(o_ref.dtype)


def _rglru(proj, conv_w, conv_b, w_ri, layer, b_r, b_i, lam, batch, seq, col, tm=256):
    tm = min(tm, seq)
    width = BRANCH_WIDTH
    nt = seq // tm
    b_ri = jnp.stack([b_r, b_i])
    return pl.pallas_call(
        _lru_kernel,
        out_shape=jax.ShapeDtypeStruct((batch * seq, width), BF16),
        grid=(batch, nt),
        in_specs=[
            pl.BlockSpec((tm, width), lambda b, t: (b * nt + t, col["lx"] // width)),
            pl.BlockSpec((HALO_ROWS, width),
                         functools.partial(_halo_index(seq // HALO_ROWS, tm // HALO_ROWS),
                                           col=col["lx"] // width)),
            pl.BlockSpec((tm, width), lambda b, t: (b * nt + t, col["lz"] // width)),
            pl.BlockSpec((CONV_WIDTH, width), lambda b, t: (0, 0)),
            pl.BlockSpec((1, width), lambda b, t: (0, 0)),
            pl.BlockSpec((None, LRU_BLOCKS, LRU_BLOCK, 2 * LRU_BLOCK), lambda b, t: (layer, 0, 0, 0)),
            pl.BlockSpec((2, width), lambda b, t: (0, 0)),
            pl.BlockSpec((1, width), lambda b, t: (0, 0)),
        ],
        out_specs=pl.BlockSpec((tm, width), lambda b, t: (b * nt + t, 0)),
        scratch_shapes=[pltpu.VMEM((HALO_ROWS + tm, width), F32),
                        pltpu.VMEM((tm, width), F32),
                        pltpu.VMEM((tm, width), F32),
                        pltpu.VMEM((HALO_ROWS, width), F32)],
        compiler_params=_params(("parallel", "arbitrary"), VMEM_LIMIT),
        name="rglru",
    )(proj, proj, proj, conv_w, conv_b.reshape(1, width), w_ri, b_ri, lam.reshape(1, width))


def _zoh(ldt, a_re, a_im):
    dt = jnp.exp(ldt)
    mag = jnp.exp(dt * a_re)
    return mag * jnp.cos(dt * a_im), mag * jnp.sin(dt * a_im)


def _s5_discretize_kernel(ldt_ref, are_ref, aim_ref, bre_ref, bim_ref, ldtg_ref, areg_ref, aimg_ref,
                          bbre_ref, bbim_ref, pre_ref, pim_ref):
    a_re = are_ref[...]
    a_im = aim_ref[...]
    ab_re, ab_im = _zoh(ldt_ref[...], a_re, a_im)
    den = a_re * a_re + a_im * a_im
    f_re = ((ab_re - 1.0) * a_re + ab_im * a_im) / den
    f_im = (ab_im * a_re - (ab_re - 1.0) * a_im) / den
    b_re = bre_ref[...]
    b_im = bim_ref[...]
    bbre_ref[...] = f_re * b_re - f_im * b_im
    bbim_ref[...] = f_re * b_im + f_im * b_re

    seg_len = pre_ref.shape[0] - 2
    ab_re, ab_im = _zoh(ldtg_ref[...], areg_ref[...], aimg_ref[...])
    p_re, p_im = ab_re, ab_im
    pre_ref[0] = p_re
    pim_ref[0] = p_im
    for p in range(1, seg_len):
        p_re, p_im = p_re * ab_re - p_im * ab_im, p_re * ab_im + p_im * ab_re
        pre_ref[p] = p_re
        pim_ref[p] = p_im
    for p in range(seg_len, seg_len + 2):
        p_re, p_im = p_re * p_re - p_im * p_im, 2.0 * (p_re * p_im)
        pre_ref[p] = p_re
        pim_ref[p] = p_im


def _s5_discretize(log_dt, a_re, a_im, b_re, b_im, seg_len):
    rows = SSM_GROUPS * SSM_GROUP
    rep = lambda p: jnp.repeat(p, SSM_GROUP, axis=0)
    ldt = jnp.broadcast_to(log_dt[:, None], (SSM_GROUPS, SSM_STATE))
    to_rows = lambda b: b.transpose(0, 2, 1).reshape(rows, SSM_STATE)
    shp = jax.ShapeDtypeStruct((rows, SSM_STATE), F32)
    pshp = jax.ShapeDtypeStruct((seg_len + 2, SSM_GROUPS, SSM_STATE), F32)
    return pl.pallas_call(
        _s5_discretize_kernel, out_shape=(shp, shp, pshp, pshp), name="s5_discretize",
    )(rep(ldt), rep(a_re), rep(a_im), to_rows(b_re), to_rows(b_im), ldt, a_re, a_im)


def _s5_kernel(u_ref, z_ref, bw_ref, cw_ref, d_ref, a1_ref, hs_ref, al_ref,
               wg_ref, bg_ref, o_ref, up, xs, y_s, carry):
    t = pl.program_id(1)
    tm = u_ref.shape[0]
    seg = tm // HALO_ROWS
    ns = SSM_BLOCK_STATES
    ch = SSM_GROUP * (SSM_GROUPS // SSM_LANE_BLOCKS)

    @pl.when(t == 0)
    def _():
        carry[...] = jnp.zeros(carry.shape, F32)

    wrow = lax.broadcasted_iota(jnp.int32, (tm, tm), 0)
    trow = lax.broadcasted_iota(jnp.int32, (tm, tm), 1)
    select = (trow == (wrow % HALO_ROWS) * seg + wrow // HALO_ROWS).astype(BF16)
    up[...] = jnp.dot(select, u_ref[...].astype(BF16), preferred_element_type=F32).astype(BF16)

    sub = lax.broadcasted_iota(jnp.int32, (HALO_ROWS, ns), 0)
    zero = jnp.zeros((HALO_ROWS, ns), F32)
    n_buf = xs.shape[0]

    def input_states(cb):
        xs[cb % n_buf] = jnp.dot(up[:, cb * ch:(cb + 1) * ch], bw_ref[cb], preferred_element_type=F32)

    def output(cb):
        cx = _dot(xs[cb % n_buf], cw_ref[cb])
        for i in range(seg):
            y_s[cb, pl.ds(i, HALO_ROWS, stride=seg), :] = cx[i * HALO_ROWS:(i + 1) * HALO_ROWS, :]

    input_states(0)
    for cb in range(SSM_LANE_BLOCKS):
        if cb + 1 < SSM_LANE_BLOCKS:
            input_states(cb + 1)
        xb = xs.at[cb % n_buf]
        ar = a1_ref[cb, :, 0:ns]
        ai = a1_ref[cb, :, ns:2 * ns]
        er, ei = zero, zero
        for i in range(seg):
            rows = slice(i * HALO_ROWS, (i + 1) * HALO_ROWS)
            er, ei = (xb[rows, 0:ns] + (ar * er - ai * ei),
                      xb[rows, ns:2 * ns] + (ar * ei + ai * er))

        fr = jnp.where(sub == 0, carry[cb, :, 0:ns], pltpu.roll(er, 1, 0))
        fi = jnp.where(sub == 0, carry[cb, :, ns:2 * ns], pltpu.roll(ei, 1, 0))
        for k, shift in enumerate((1, 2, 4)):
            mr = hs_ref[k, cb, :, 0:ns]
            mi = hs_ref[k, cb, :, ns:2 * ns]
            pr = pltpu.roll(fr, shift, 0)
            pi = pltpu.roll(fi, shift, 0)
            fr, fi = fr + (mr * pr - mi * pi), fi + (mr * pi + mi * pr)
        lr = al_ref[cb, :, 0:ns]
        li = al_ref[cb, :, ns:2 * ns]
        last = slice(HALO_ROWS - 1, HALO_ROWS)
        carry[cb, :, 0:ns] = jnp.broadcast_to((lr * fr - li * fi + er)[last, :], (HALO_ROWS, ns))
        carry[cb, :, ns:2 * ns] = jnp.broadcast_to((lr * fi + li * fr + ei)[last, :], (HALO_ROWS, ns))

        for i in range(seg):
            rows = slice(i * HALO_ROWS, (i + 1) * HALO_ROWS)
            fr, fi = (xb[rows, 0:ns] + (ar * fr - ai * fi),
                      xb[rows, ns:2 * ns] + (ar * fi + ai * fr))
            xb[rows, 0:ns] = fr
            xb[rows, ns:2 * ns] = fi
        output(cb)

    y = jnp.concatenate([y_s[cb] for cb in range(SSM_LANE_BLOCKS)], axis=-1) + d_ref[...] * u_ref[...]
    y = 0.5 * y * (1.0 + jnp.tanh(math.sqrt(2.0 / math.pi) * (y + 0.044715 * (y * y * y))))
    glu = _dot(y, wg_ref[...]) + bg_ref[...]
    width = y.shape[1]
    o_ref[...] = (glu[:, :width] * _sigmoid(glu[:, width:]) * _silu(z_ref[...])).astype(o_ref.dtype)


def _s5(proj, log_dt, a_re, a_im, b_re, b_im, c_re, c_im, d_skip, w_glu, layer, b_glu, batch, seq, col, tm=256):
    tm = min(tm, seq)
    width = BRANCH_WIDTH
    nt = seq // tm
    seg = tm // HALO_ROWS
    nb = SSM_LANE_BLOCKS
    gb = SSM_GROUPS // nb
    ns = SSM_BLOCK_STATES
    bb_re, bb_im, p_re, p_im = _s5_discretize(log_dt, a_re, a_im, b_re, b_im, seg)
    eye = jnp.eye(gb, dtype=F32)

    def in_block(bb):
        bb = bb.reshape(nb, gb, SSM_GROUP, SSM_STATE)
        return jnp.einsum("bgcn,gh->bgchn", bb, eye).reshape(nb, gb * SSM_GROUP, ns)

    def out_block(c):
        c = c.reshape(nb, gb, SSM_GROUP, SSM_STATE)
        return jnp.einsum("bgcn,gh->bgnhc", c, eye).reshape(nb, ns, gb * SSM_GROUP)

    bw = jnp.concatenate([in_block(bb_re), in_block(bb_im)], axis=-1).astype(BF16)
    cw = jnp.concatenate([out_block(c_re), -out_block(c_im)], axis=1).astype(BF16)

    def lanes(p):
        return p.reshape(p.shape[0], nb, ns).transpose(1, 0, 2)

    sub = jnp.arange(HALO_ROWS)[:, None, None]

    def rows8(p_re_1, p_im_1, first_row=0):
        pr = jnp.where(sub >= first_row, p_re_1[None], 0.0)
        pi = jnp.where(sub >= first_row, p_im_1[None], 0.0)
        return jnp.concatenate([lanes(pr), lanes(pi)], axis=-1)

    a1 = rows8(p_re[0], p_im[0])
    al = rows8(p_re[seg - 1], p_im[seg - 1])
    hs = jnp.stack([rows8(p_re[seg - 1], p_im[seg - 1], 1), rows8(p_re[seg], p_im[seg], 2),
                    rows8(p_re[seg + 1], p_im[seg + 1], 4)])
    mult_spec = pl.BlockSpec((nb, HALO_ROWS, 2 * ns), lambda b, t: (0, 0, 0))
    return pl.pallas_call(
        _s5_kernel,
        out_shape=jax.ShapeDtypeStruct((batch * seq, width), BF16),
        grid=(batch, nt),
        in_specs=[
            pl.BlockSpec((tm, width), lambda b, t: (b * nt + t, col["su"] // width)),
            pl.BlockSpec((tm, width), lambda b, t: (b * nt + t, col["sz"] // width)),
            pl.BlockSpec((nb, gb * SSM_GROUP, 2 * ns), lambda b, t: (0, 0, 0)),
            pl.BlockSpec((nb, 2 * ns, gb * SSM_GROUP), lambda b, t: (0, 0, 0)),
            pl.BlockSpec((1, width), lambda b, t: (0, 0)),
            mult_spec,
            pl.BlockSpec((3, nb, HALO_ROWS, 2 * ns), lambda b, t: (0, 0, 0, 0)),
            mult_spec,
            pl.BlockSpec((None, width, 2 * width), lambda b, t: (layer, 0, 0)),
            pl.BlockSpec((1, 2 * width), lambda b, t: (0, 0)),
        ],
        out_specs=pl.BlockSpec((tm, width), lambda b, t: (b * nt + t, 0)),
        scratch_shapes=[pltpu.VMEM((tm, width), BF16),
                        pltpu.VMEM((2, tm, 2 * ns), F32),
                        pltpu.VMEM((nb, tm, gb * SSM_GROUP), F32),
                        pltpu.VMEM((nb, HALO_ROWS, 2 * ns), F32)],
        compiler_params=_params(("parallel", "arbitrary"), VMEM_LIMIT),
        name="s5",
    )(proj, proj, bw, cw, d_skip.reshape(1, width), a1, hs, al, w_glu, b_glu.reshape(1, 2 * width))


def _mem_attn_kernel(q_ref, z_ref, k_ref, v_ref, o_ref):
    dh = MEM_HEAD_DIM
    z = z_ref[...]
    for h in range(MEM_HEADS):
        cols = slice(h * dh, (h + 1) * dh)
        s = _dot_nt(q_ref[:, cols], k_ref[:, cols]) * (dh ** -0.5)
        s = s - jnp.max(s, axis=-1, keepdims=True)
        p = jnp.exp(s)
        p = p / jnp.sum(p, axis=-1, keepdims=True)
        o = _dot(p, v_ref[:, cols])
        o_ref[:, cols] = (o * _silu(z[:, cols])).astype(o_ref.dtype)


def _mem_attn(proj, kv, batch, seq, mem_len, col, tm=256):
    tm = min(tm, seq)
    width = BRANCH_WIDTH
    nt = seq // tm
    return pl.pallas_call(
        _mem_attn_kernel,
        out_shape=jax.ShapeDtypeStruct((batch * seq, width), BF16),
        grid=(batch, nt),
        in_specs=[
            pl.BlockSpec((tm, width), lambda b, t: (b * nt + t, col["mq"] // width)),
            pl.BlockSpec((tm, width), lambda b, t: (b * nt + t, col["mz"] // width)),
            pl.BlockSpec((mem_len, width), lambda b, t: (b, 0)),
            pl.BlockSpec((mem_len, width), lambda b, t: (b, 1)),
        ],
        out_specs=pl.BlockSpec((tm, width), lambda b, t: (b * nt + t, 0)),
        compiler_params=_params(("parallel", "parallel"), VMEM_LIMIT),
        name="mem_attn",
    )(proj, proj, kv, kv)


def _merge_kernel(g_ref, oa_ref, ob_ref, oc_ref, od_ref, wg_ref, bg_ref, wb_ref, o_ref):
    g_low = g_ref[...]
    acc = None
    for n, br_ref in enumerate((oa_ref, ob_ref, oc_ref, od_ref)):
        gate = _sigmoid(_dot(g_low, wg_ref[n]) + bg_ref[n:n + 1, :])
        term = gate * jnp.dot(br_ref[...], wb_ref[n], preferred_element_type=F32)
        acc = term if acc is None else acc + term
    o_ref[...] = acc.astype(o_ref.dtype)


def _merge(proj, branches, w_gate, layer, b_gate, w_branch, col, tm=1024, tn=512):
    m = proj.shape[0]
    d_model = w_gate.shape[-1]
    tm = min(tm, m)
    br_spec = pl.BlockSpec((tm, BRANCH_WIDTH), lambda i, j: (i, 0))
    return pl.pallas_call(
        _merge_kernel,
        out_shape=jax.ShapeDtypeStruct((m, d_model), BF16),
        grid=(m // tm, d_model // tn),
        in_specs=[
            pl.BlockSpec((tm, GATE_RANK), lambda i, j: (i, col["glow"] // GATE_RANK)),
            br_spec, br_spec, br_spec, br_spec,
            pl.BlockSpec((None, N_BRANCH, GATE_RANK, tn), lambda i, j: (layer, 0, 0, j)),
            pl.BlockSpec((N_BRANCH, tn), lambda i, j: (0, j)),
            pl.BlockSpec((None, N_BRANCH, BRANCH_WIDTH, tn), lambda i, j: (layer, 0, 0, j)),
        ],
        out_specs=pl.BlockSpec((tm, tn), lambda i, j: (i, j)),
        compiler_params=_params(("parallel", "arbitrary"), VMEM_LIMIT),
        name="merge",
    )(proj, *branches, w_gate, b_gate, w_branch)


RELAYOUT_COLS = 512


def _relayout_kernel(cur_ref, nxt_ref, small_ref, o_ref, prev, *, n_aligned, shift, tail):
    c = pl.program_id(2)
    last = pl.num_programs(2) - 1
    rows, tk = cur_ref.shape

    def joined(left):
        return jnp.concatenate([left[shift:, :], nxt_ref[0:shift, :]], axis=0)

    def emit(block):
        o_ref[...] = block.T.astype(o_ref.dtype)

    @pl.when(c < n_aligned)
    def _():
        emit(cur_ref[...])

    @pl.when(c == n_aligned)
    def _():
        emit(joined(cur_ref[...]))

    @pl.when((c > n_aligned) & (c < last))
    def _():
        emit(joined(prev[...]))

    @pl.when(c == last)
    def _():
        emit(jnp.concatenate([prev[shift:shift + tail, :], small_ref[0:shift, :],
                              jnp.zeros((rows - tail - shift, tk), F32)], axis=0))

    @pl.when(c >= n_aligned)
    def _():
        prev[...] = nxt_ref[...]


def _in_proj_layout(w_in, tk=1024):
    depth, d_model, d_in = w_in.shape
    cb = RELAYOUT_COLS
    a_end = 4 * BRANCH_WIDTH
    small = 2 * DN_HEADS
    main = 6 * BRANCH_WIDTH + GATE_RANK
    assert d_in == a_end + small + main and a_end % cb == 0
    n_aligned = a_end // cb
    n_out = -(-d_in // cb)
    tail = main - (n_out - 1 - n_aligned) * cb
    assert 0 < tail and tail + small <= cb
    n_in = -(-d_in // cb)
    tk = min(tk, d_model)
    w_t = jnp.transpose(w_in, (0, 2, 1))
    blk = lambda index: pl.BlockSpec((None, cb, tk), index)
    w = pl.pallas_call(
        functools.partial(_relayout_kernel, n_aligned=n_aligned, shift=small, tail=tail),
        out_shape=jax.ShapeDtypeStruct((depth, d_model, n_out * cb), BF16),
        grid=(depth, d_model // tk, n_out),
        in_specs=[blk(lambda l, k, c: (l, jnp.minimum(c, n_aligned), k)),
                  blk(lambda l, k, c: (l, jnp.clip(c + 1, n_aligned, n_in - 1), k)),
                  blk(lambda l, k, c: (l, n_aligned, k))],
        out_specs=pl.BlockSpec((None, tk, cb), lambda l, k, c: (l, k, c)),
        scratch_shapes=[pltpu.VMEM((cb, tk), F32)],
        compiler_params=_params(("parallel", "parallel", "arbitrary")),
        name="in_proj_relayout",
    )(w_t, w_t, w_t)
    w1 = BRANCH_WIDTH
    col = {"qkv": 0, "dz": 3 * w1, "lx": 4 * w1, "lz": 5 * w1, "su": 6 * w1, "sz": 7 * w1,
           "mq": 8 * w1, "mz": 9 * w1, "glow": 10 * w1, "small": 10 * w1 + GATE_RANK}
    return w, col


def kernel(x, mem, norm_w, w_in, dn_conv_w, dn_a_log, dn_dt_bias, dn_norm_w, lru_conv_w, lru_conv_b, lru_w_r, lru_b_r, lru_w_i, lru_b_i, lru_lambda, ssm_log_dt, ssm_a_re, ssm_a_im, ssm_b_re, ssm_b_im, ssm_c_re, ssm_c_im, ssm_d, ssm_w_glu, ssm_b_glu, mem_norm_w, w_kv, w_gate, b_gate, w_branch, w_out, final_norm_w):
    batch, seq, d_model = x.shape
    mem_len = mem.shape[1]
    depth = w_in.shape[0]
    xf = x.reshape(batch * seq, d_model)
    memf = mem.reshape(batch * mem_len, d_model)
    w_cat, col = _in_proj_layout(w_in)
    w_kv_b = w_kv.astype(BF16)
    w_gate_b = w_gate.astype(BF16)
    w_branch_b = w_branch.astype(BF16)
    w_out_b = w_out.astype(BF16)
    w_glu_b = ssm_w_glu.astype(BF16)
    w_ri = jnp.concatenate([lru_w_r, lru_w_i], axis=-1).astype(BF16)
    h = _rmsnorm(xf, norm_w[0], BF16)
    for l in range(depth):
        proj = _matmul(h, w_cat, l, F32)
        o_a = _deltanet(proj, dn_conv_w[l], dn_a_log[l], dn_dt_bias[l], dn_norm_w[l], batch, seq, col)
        o_b = _rglru(proj, lru_conv_w[l], lru_conv_b[l], w_ri, l, lru_b_r[l], lru_b_i[l],
                     lru_lambda[l], batch, seq, col)
        o_c = _s5(proj, ssm_log_dt[l], ssm_a_re[l], ssm_a_im[l], ssm_b_re[l], ssm_b_im[l],
                  ssm_c_re[l], ssm_c_im[l], ssm_d[l].reshape(-1), w_glu_b, l, ssm_b_glu[l], batch, seq, col)
        m_n = _rmsnorm(memf, mem_norm_w[l], BF16)
        kv = _matmul(m_n, w_kv_b, l, BF16)
        o_d = _mem_attn(proj, kv, batch, seq, mem_len, col)
        merged = _merge(proj, (o_a, o_b, o_c, o_d), w_gate_b, l, b_gate[l], w_branch_b, col)
        last = l == depth - 1
        next_norm_w = final_norm_w if last else norm_w[l + 1]
        xf, h = _out_proj_norm(merged, w_out_b, l, xf, next_norm_w, write_x=not last,
                               h_dtype=F32 if last else BF16)
    return h.reshape(batch, seq, d_model)
```

```python
import functools
import math

import jax
import jax.numpy as jnp
from jax import lax
from jax.experimental import pallas as pl
from jax.experimental.pallas import tpu as pltpu

F32 = jnp.float32
BF16 = jnp.bfloat16
HIGHEST = lax.Precision.HIGHEST

NORM_EPS = 1e-6
CONV_WIDTH = 4
HALO_ROWS = 8
N_BRANCH = 4
BRANCH_WIDTH = 1024
DN_HEAD_DIM = 128
DN_HEADS = 8
DN_CHUNK = 64
DN_TILE = 128
LRU_BLOCKS = 8
LRU_BLOCK = 128
LRU_C = 8.0
SSM_GROUP = 16
SSM_GROUPS = 64
SSM_STATE = 64
SSM_LANE_BLOCKS = 8
SSM_BLOCK_STATES = 512
MEM_HEADS = 4
MEM_HEAD_DIM = 256
GATE_RANK = 256
SMALL_COLS = 128

VMEM_LIMIT = 56 * 1024 * 1024


def _params(sem, vmem=None):
    return pltpu.CompilerParams(dimension_semantics=sem, vmem_limit_bytes=vmem)


def _sigmoid(x):
    return 0.5 * jnp.tanh(0.5 * x) + 0.5


def _silu(x):
    half = 0.5 * x
    return half * jnp.tanh(half) + half


def _softplus(x):
    return jnp.maximum(x, 0.0) + jnp.log1p(jnp.exp(-jnp.abs(x)))


def _dot(a, b):
    return jnp.dot(a.astype(BF16), b.astype(BF16), preferred_element_type=F32)


def _dot_nt(a, b):
    return lax.dot_general(a.astype(BF16), b.astype(BF16), (((1,), (1,)), ((), ())),
                           preferred_element_type=F32)


def _dot_f32(a, b):
    return jnp.dot(a, b, precision=HIGHEST, preferred_element_type=F32)


def _rmsnorm_kernel(x_ref, w_ref, o_ref):
    x = x_ref[...]
    var = jnp.mean(x * x, axis=-1, keepdims=True)
    o_ref[...] = (x * lax.rsqrt(var + NORM_EPS) * w_ref[...]).astype(o_ref.dtype)


def _rmsnorm(x, w, out_dtype, tm=256):
    m, d = x.shape
    tm = min(tm, m)
    return pl.pallas_call(
        _rmsnorm_kernel,
        out_shape=jax.ShapeDtypeStruct((m, d), out_dtype),
        grid=(m // tm,),
        in_specs=[pl.BlockSpec((tm, d), lambda i: (i, 0)),
                  pl.BlockSpec((1, d), lambda i: (0, 0))],
        out_specs=pl.BlockSpec((tm, d), lambda i: (i, 0)),
        compiler_params=_params(("parallel",)),
        name="rmsnorm",
    )(x, w.reshape(1, d))


def _mm_kernel(a_ref, w_ref, o_ref):
    o_ref[...] = jnp.dot(a_ref[...], w_ref[...].astype(BF16), preferred_element_type=F32).astype(o_ref.dtype)


def _matmul(a, w, layer, out_dtype, tm=1024, tn=512):
    m, k = a.shape
    n = w.shape[2]
    tm = min(tm, m)
    tn = min(tn, n)
    return pl.pallas_call(
        _mm_kernel,
        out_shape=jax.ShapeDtypeStruct((m, n), out_dtype),
        grid=(m // tm, n // tn),
        in_specs=[pl.BlockSpec((tm, k), lambda i, j: (i, 0)),
                  pl.BlockSpec((None, k, tn), lambda i, j: (layer, 0, j))],
        out_specs=pl.BlockSpec((tm, tn), lambda i, j: (i, j)),
        compiler_params=_params(("parallel", "arbitrary"), VMEM_LIMIT),
        name="matmul",
    )(a, w)


def _out_norm_kernel(a_ref, w_ref, r_ref, nw_ref, *refs, write_x):
    if write_x:
        x_ref, h_ref, acc = refs
    else:
        h_ref, acc = refs
    j = pl.program_id(1)
    nj, _, tn = acc.shape
    acc[j] = r_ref[...] + jnp.dot(a_ref[...], w_ref[...], preferred_element_type=F32)

    @pl.when(j == nj - 1)
    def _():
        ssq = None
        for jj in range(nj):
            blk = acc[jj]
            part = jnp.sum(blk * blk, axis=-1, keepdims=True)
            ssq = part if ssq is None else ssq + part
        inv = lax.rsqrt(ssq * (1.0 / (nj * tn)) + NORM_EPS)
        for jj in range(nj):
            blk = acc[jj]
            cols = slice(jj * tn, (jj + 1) * tn)
            if write_x:
                x_ref[:, cols] = blk
            h_ref[:, cols] = (blk * inv * nw_ref[:, cols]).astype(h_ref.dtype)


def _out_proj_norm(a, w, layer, residual, norm_w, write_x, h_dtype, tm=512, tn=512):
    m, k = a.shape
    n = w.shape[2]
    tm = min(tm, m)
    row_spec = pl.BlockSpec((tm, n), lambda i, j: (i, 0))
    out_shape = [jax.ShapeDtypeStruct((m, n), h_dtype)]
    out_specs = [row_spec]
    if write_x:
        out_shape.insert(0, jax.ShapeDtypeStruct((m, n), F32))
        out_specs.insert(0, row_spec)
    outs = pl.pallas_call(
        functools.partial(_out_norm_kernel, write_x=write_x),
        out_shape=out_shape,
        grid=(m // tm, n // tn),
        in_specs=[pl.BlockSpec((tm, k), lambda i, j: (i, 0)),
                  pl.BlockSpec((None, k, tn), lambda i, j: (layer, 0, j)),
                  pl.BlockSpec((tm, tn), lambda i, j: (i, j)),
                  pl.BlockSpec((1, n), lambda i, j: (0, 0))],
        out_specs=out_specs,
        scratch_shapes=[pltpu.VMEM((n // tn, tm, tn), F32)],
        compiler_params=_params(("parallel", "arbitrary"), VMEM_LIMIT),
        name="out_proj_norm",
    )(a, w, residual, norm_w.reshape(1, n))
    return (outs[0], outs[1]) if write_x else (None, outs[0])


def _fill_conv_buffer(buf, halo_ref, x_ref, t, tm):
    @pl.when(t == 0)
    def _():
        buf[0:HALO_ROWS, :] = jnp.zeros((HALO_ROWS, buf.shape[1]), F32)

    @pl.when(t != 0)
    def _():
        buf[0:HALO_ROWS, :] = halo_ref[...]

    buf[HALO_ROWS:HALO_ROWS + tm, :] = x_ref[...]


def _causal_conv(buf, cw_ref, tm):
    assert CONV_WIDTH == 4
    x = buf[...]
    x1 = pltpu.roll(x, 1, 0)
    early = cw_ref[0:1, :] * x1 + cw_ref[1:2, :] * x
    late = cw_ref[2:3, :] * x1 + cw_ref[3:4, :] * x
    return (pltpu.roll(early, 2, 0) + late)[HALO_ROWS:HALO_ROWS + tm, :]


def _halo_index(seq_blocks, tile_blocks):
    def index(b, t, col):
        return (jnp.maximum(b * seq_blocks + t * tile_blocks - 1, 0), col)
    return index


def _dn_kernel(qkv_ref, halo_ref, z_ref, sm_ref, cw_ref, alog_ref, dtb_ref, nw_ref, o_ref,
               buf, state):
    t = pl.program_id(1)
    slots, tm, _ = qkv_ref.shape
    d = DN_HEAD_DIM

    @pl.when(t == 0)
    def _():
        state[...] = jnp.zeros(state.shape, F32)

    for slot in range(slots):
        _fill_conv_buffer(buf.at[slot], halo_ref.at[slot], qkv_ref.at[slot], t, tm)

    row = lax.broadcasted_iota(jnp.int32, (tm, tm), 0)
    col = lax.broadcasted_iota(jnp.int32, (tm, tm), 1)
    same_chunk = (row // DN_CHUNK) == (col // DN_CHUNK)
    causal = same_chunk & (row >= col)
    strict = same_chunk & (row > col)
    causal_f = causal.astype(F32)
    same_chunk_f = same_chunk.astype(F32)

    heads = range(DN_HEADS)
    chunks = range(tm // DN_CHUNK)
    zero_chunk = jnp.zeros((DN_CHUNK, 2 * d), F32)
    nw = nw_ref[...]

    def prepare(slot):
        qkv = _silu(_causal_conv(buf.at[slot], cw_ref, tm))
        small = sm_ref[slot]
        g_all = -jnp.exp(alog_ref[...]) * _softplus(small + dtb_ref[...])
        gc_all = _dot_f32(causal_f, g_all)
        return dict(qkv=qkv, beta=_sigmoid(small), gc=gc_all,
                    gl=_dot_f32(same_chunk_f, g_all),
                    gc_rows=gc_all.T)

    def head_inputs(ctx, h):
        qkv = ctx["qkv"]
        qh = qkv[:, h * d:(h + 1) * d]
        kh = qkv[:, BRANCH_WIDTH + h * d:BRANCH_WIDTH + (h + 1) * d]
        vh = qkv[:, 2 * BRANCH_WIDTH + h * d:2 * BRANCH_WIDTH + (h + 1) * d]
        qh = qh * (lax.rsqrt(jnp.sum(qh * qh, axis=-1, keepdims=True) + NORM_EPS) * (d ** -0.5))
        kh = kh * lax.rsqrt(jnp.sum(kh * kh, axis=-1, keepdims=True) + NORM_EPS)
        beta = ctx["beta"][:, h:h + 1]
        gc = ctx["gc"][:, DN_HEADS + h:DN_HEADS + h + 1]
        gl = ctx["gl"][:, DN_HEADS + h:DN_HEADS + h + 1]
        gr = ctx["gc_rows"][DN_HEADS + h:DN_HEADS + h + 1, :]
        eg = jnp.exp(gc)
        kbh = kh * beta
        return dict(
            q=qh.astype(BF16), k=kh.astype(BF16), kb=kbh.astype(BF16), qd=qh * eg,
            kd_t=(kh * jnp.exp(gl - gc)).T.astype(BF16),
            decay=jnp.where(causal, jnp.exp(jnp.where(causal, gc - gr, 0.0)), 0.0),
            sol=jnp.concatenate([vh * beta, kbh * eg], axis=-1),
            gl=gl)

    def solve_and_scan(slot, hd, between):
        x, qk = [], []
        for h in heads:
            both = lax.dot_general(jnp.concatenate([hd[h]["kb"], hd[h]["q"]], axis=0), hd[h]["k"],
                                   (((1,), (1,)), ((), ())), preferred_element_type=F32)
            x.append(jnp.where(strict, -(both[:tm] * hd[h]["decay"]), 0.0))
            qk.append(jnp.where(causal, both[tm:] * hd[h]["decay"], 0.0).astype(BF16))
        between()

        sol = [hd[h]["sol"] for h in heads]
        n_square = 5
        for it in range(n_square + 1):
            for h in heads:
                xb = x[h].astype(BF16)
                sb = sol[h].astype(BF16)
                if it < n_square:
                    y = jnp.dot(xb, jnp.concatenate([xb, sb], axis=-1), preferred_element_type=F32)
                    x[h] = y[:, :tm]
                    sol[h] = sol[h] + y[:, tm:]
                else:
                    sol[h] = sol[h] + jnp.dot(xb, sb, preferred_element_type=F32)
            between()

        kw, ku = [], []
        for h in heads:
            kw_h, ku_h = [], []
            for c in chunks:
                rows = slice(c * DN_CHUNK, (c + 1) * DN_CHUNK)
                wu = jnp.concatenate([sol[h][rows, d:], sol[h][rows, :d]], axis=-1)
                padded = jnp.concatenate([wu if i == c else zero_chunk for i in chunks], axis=0)
                prod = jnp.dot(hd[h]["kd_t"], padded.astype(BF16), preferred_element_type=F32)
                kw_h.append(prod[:, :d].astype(BF16))
                ku_h.append(prod[:, d:])
            kw.append(kw_h)
            ku.append(ku_h)
        between()

        s = [state[slot, h] for h in heads]
        inter = [[] for _ in heads]
        v_new = [[] for _ in heads]
        for c in chunks:
            rows = slice(c * DN_CHUNK, (c + 1) * DN_CHUNK)
            for h in heads:
                sb = s[h].astype(BF16)
                lhs = jnp.concatenate([hd[h]["qd"][rows], sol[h][rows, d:]], axis=0).astype(BF16)
                r = jnp.dot(lhs, sb, preferred_element_type=F32)
                inter[h].append(r[:DN_CHUNK])
                v_new[h].append(sol[h][rows, :d] - r[DN_CHUNK:])
                s[h] = (s[h] * jnp.exp(hd[h]["gl"][c * DN_CHUNK:c * DN_CHUNK + 1, :])
                        - jnp.dot(kw[h][c], sb, preferred_element_type=F32) + ku[h][c])
            between()
        z = z_ref[slot]
        for h in heads:
            state[slot, h] = s[h]
            out = jnp.concatenate(inter[h], axis=0) + jnp.dot(
                qk[h], jnp.concatenate(v_new[h], axis=0).astype(BF16), preferred_element_type=F32)
            var = jnp.mean(out * out, axis=-1, keepdims=True)
            out = out * lax.rsqrt(var + NORM_EPS) * nw
            o_ref[slot, :, h * d:(h + 1) * d] = (out * _silu(z[:, h * d:(h + 1) * d])).astype(o_ref.dtype)

    ctx = prepare(0)
    inputs = [head_inputs(ctx, h) for h in heads]
    for slot in range(slots):
        following = []
        pieces = []
        if slot + 1 < slots:
            pieces = [lambda nxt=slot + 1: following.append(prepare(nxt))] + [
                (lambda h=h: following.append(head_inputs(following[0], h))) for h in heads]

        def between(pieces=pieces):
            if pieces:
                pieces.pop(0)()

        solve_and_scan(slot, inputs, between)
        while pieces:
            pieces.pop(0)()
        inputs = following[1:]


def _deltanet(proj, conv_w, a_log, dt_bias, norm_w, batch, seq, col):
    tm = DN_TILE
    width = 3 * BRANCH_WIDTH
    slots = 2 if batch % 2 == 0 else 1
    pad = (0, SMALL_COLS - 2 * DN_HEADS)
    alog = jnp.pad(jnp.concatenate([jnp.zeros((DN_HEADS,), F32), a_log]), pad).reshape(1, SMALL_COLS)
    dtb = jnp.pad(jnp.concatenate([jnp.zeros((DN_HEADS,), F32), dt_bias]), pad).reshape(1, SMALL_COLS)
    nt = seq // tm
    proj3 = proj.reshape(batch, seq, proj.shape[1])
    halo_blocks = tm // HALO_ROWS
    out = pl.pallas_call(
        _dn_kernel,
        out_shape=jax.ShapeDtypeStruct((batch, seq, BRANCH_WIDTH), BF16),
        grid=(batch // slots, nt),
        in_specs=[
            pl.BlockSpec((slots, tm, width), lambda b, t: (b, t, col["qkv"] // width)),
            pl.BlockSpec((slots, HALO_ROWS, width),
                         lambda b, t: (b, jnp.maximum(t * halo_blocks - 1, 0), col["qkv"] // width)),
            pl.BlockSpec((slots, tm, BRANCH_WIDTH), lambda b, t: (b, t, col["dz"] // BRANCH_WIDTH)),
            pl.BlockSpec((slots, tm, SMALL_COLS), lambda b, t: (b, t, col["small"] // SMALL_COLS)),
            pl.BlockSpec((CONV_WIDTH, width), lambda b, t: (0, 0)),
            pl.BlockSpec((1, SMALL_COLS), lambda b, t: (0, 0)),
            pl.BlockSpec((1, SMALL_COLS), lambda b, t: (0, 0)),
            pl.BlockSpec((1, DN_HEAD_DIM), lambda b, t: (0, 0)),
        ],
        out_specs=pl.BlockSpec((slots, tm, BRANCH_WIDTH), lambda b, t: (b, t, 0)),
        scratch_shapes=[pltpu.VMEM((slots, HALO_ROWS + tm, width), F32),
                        pltpu.VMEM((slots, DN_HEADS, DN_HEAD_DIM, DN_HEAD_DIM), F32)],
        compiler_params=_params(("parallel", "arbitrary"), VMEM_LIMIT),
        name="deltanet",
    )(proj3, proj3, proj3, proj3, conv_w, alog, dtb, norm_w.reshape(1, DN_HEAD_DIM))
    return out.reshape(batch * seq, BRANCH_WIDTH)


def _lru_kernel(x_ref, halo_ref, z_ref, cw_ref, cb_ref, w_ref, b_ref, lam_ref, o_ref,
                buf, a_s, b_s, carry):
    t = pl.program_id(1)
    tm = x_ref.shape[0]
    width = x_ref.shape[1]

    @pl.when(t == 0)
    def _():
        carry[...] = jnp.zeros(carry.shape, F32)

    _fill_conv_buffer(buf, halo_ref, x_ref, t, tm)
    xc = _causal_conv(buf, cw_ref, tm) + cb_ref[...]
    gates = []
    for n in range(LRU_BLOCKS):
        gates.append(_dot(xc[:, n * LRU_BLOCK:(n + 1) * LRU_BLOCK], w_ref[n]))
    r = jnp.concatenate([g[:, :LRU_BLOCK] for g in gates], axis=-1)
    i = jnp.concatenate([g[:, LRU_BLOCK:] for g in gates], axis=-1)
    r = _sigmoid(r + b_ref[0:1, :])
    i = _sigmoid(i + b_ref[1:2, :])
    log_a = -LRU_C * r * _softplus(-lam_ref[...])
    a = jnp.exp(log_a)
    a_s[...] = a
    one_minus_a2 = 1.0 - a * a
    root = jnp.where(one_minus_a2 > 0.0, one_minus_a2 * lax.rsqrt(one_minus_a2), 0.0)
    b_s[...] = root * (i * xc)

    sub = lax.broadcasted_iota(jnp.int32, (HALO_ROWS, width), 0)

    def group(g, h_prev):
        rows = pl.ds(pl.multiple_of(g * HALO_ROWS, HALO_ROWS), HALO_ROWS)
        a = a_s[rows, :]
        b = b_s[rows, :]
        for shift in (1, 2, 4):
            keep = sub >= shift
            a_prev = jnp.where(keep, pltpu.roll(a, shift, 0), 1.0)
            b_prev = jnp.where(keep, pltpu.roll(b, shift, 0), 0.0)
            b = a * b_prev + b
            a = a * a_prev
        h = a * h_prev + b
        b_s[rows, :] = h
        return jnp.broadcast_to(h[HALO_ROWS - 1:HALO_ROWS, :], (HALO_ROWS, width))

    carry[...] = lax.fori_loop(0, tm // HALO_ROWS, group, carry[...])
    o_ref[...] = (b_s[...] * _silu(z_ref[...])).astype(o_ref.dtype)


def _rglru(proj, conv_w, conv_b, w_ri, layer, b_r, b_i, lam, batch, seq, col, tm=256):
    tm = min(tm, seq)
    width = BRANCH_WIDTH
    nt = seq // tm
    b_ri = jnp.stack([b_r, b_i])
    return pl.pallas_call(
        _lru_kernel,
        out_shape=jax.ShapeDtypeStruct((batch * seq, width), BF16),
        grid=(batch, nt),
        in_specs=[
            pl.BlockSpec((tm, width), lambda b, t: (b * nt + t, col["lx"] // width)),
            pl.BlockSpec((HALO_ROWS, width),
                         functools.partial(_halo_index(seq // HALO_ROWS, tm // HALO_ROWS),
                                           col=col["lx"] // width)),
            pl.BlockSpec((tm, width), lambda b, t: (b * nt + t, col["lz"] // width)),
            pl.BlockSpec((CONV_WIDTH, width), lambda b, t: (0, 0)),
            pl.BlockSpec((1, width), lambda b, t: (0, 0)),
            pl.BlockSpec((None, LRU_BLOCKS, LRU_BLOCK, 2 * LRU_BLOCK), lambda b, t: (layer, 0, 0, 0)),
            pl.BlockSpec((2, width), lambda b, t: (0, 0)),
            pl.BlockSpec((1, width), lambda b, t: (0, 0)),
        ],
        out_specs=pl.BlockSpec((tm, width), lambda b, t: (b * nt + t, 0)),
        scratch_shapes=[pltpu.VMEM((HALO_ROWS + tm, width), F32),
                        pltpu.VMEM((tm, width), F32),
                        pltpu.VMEM((tm, width), F32),
                        pltpu.VMEM((HALO_ROWS, width), F32)],
        compiler_params=_params(("parallel", "arbitrary"), VMEM_LIMIT),
        name="rglru",
    )(proj, proj, proj, conv_w, conv_b.reshape(1, width), w_ri, b_ri, lam.reshape(1, width))


def _zoh(ldt, a_re, a_im):
    dt = jnp.exp(ldt)
    mag = jnp.exp(dt * a_re)
    return mag * jnp.cos(dt * a_im), mag * jnp.sin(dt * a_im)


def _s5_discretize_kernel(ldt_ref, are_ref, aim_ref, bre_ref, bim_ref, ldtg_ref, areg_ref, aimg_ref,
                          bbre_ref, bbim_ref, pre_ref, pim_ref):
    a_re = are_ref[...]
    a_im = aim_ref[...]
    ab_re, ab_im = _zoh(ldt_ref[...], a_re, a_im)
    den = a_re * a_re + a_im * a_im
    f_re = ((ab_re - 1.0) * a_re + ab_im * a_im) / den
    f_im = (ab_im * a_re - (ab_re - 1.0) * a_im) / den
    b_re = bre_ref[...]
    b_im = bim_ref[...]
    bbre_ref[...] = f_re * b_re - f_im * b_im
    bbim_ref[...] = f_re * b_im + f_im * b_re

    seg_len = pre_ref.shape[0] - 2
    ab_re, ab_im = _zoh(ldtg_ref[...], areg_ref[...], aimg_ref[...])
    p_re, p_im = ab_re, ab_im
    pre_ref[0] = p_re
    pim_ref[0] = p_im
    for p in range(1, seg_len):
        p_re, p_im = p_re * ab_re - p_im * ab_im, p_re * ab_im + p_im * ab_re
        pre_ref[p] = p_re
        pim_ref[p] = p_im
    for p in range(seg_len, seg_len + 2):
        p_re, p_im = p_re * p_re - p_im * p_im, 2.0 * (p_re * p_im)
        pre_ref[p] = p_re
        pim_ref[p] = p_im


def _s5_discretize(log_dt, a_re, a_im, b_re, b_im, seg_len):
    rows = SSM_GROUPS * SSM_GROUP
    rep = lambda p: jnp.repeat(p, SSM_GROUP, axis=0)
    ldt = jnp.broadcast_to(log_dt[:, None], (SSM_GROUPS, SSM_STATE))
    to_rows = lambda b: b.transpose(0, 2, 1).reshape(rows, SSM_STATE)
    shp = jax.ShapeDtypeStruct((rows, SSM_STATE), F32)
    pshp = jax.ShapeDtypeStruct((seg_len + 2, SSM_GROUPS, SSM_STATE), F32)
    return pl.pallas_call(
        _s5_discretize_kernel, out_shape=(shp, shp, pshp, pshp), name="s5_discretize",
    )(rep(ldt), rep(a_re), rep(a_im), to_rows(b_re), to_rows(b_im), ldt, a_re, a_im)


def _s5_kernel(u_ref, z_ref, bw_ref, cw_ref, d_ref, a1_ref, hs_ref, al_ref,
               wg_ref, bg_ref, o_ref, up, xs, y_s, carry):
    t = pl.program_id(1)
    tm = u_ref.shape[0]
    seg = tm // HALO_ROWS
    ns = SSM_BLOCK_STATES
    ch = SSM_GROUP * (SSM_GROUPS // SSM_LANE_BLOCKS)

    @pl.when(t == 0)
    def _():
        carry[...] = jnp.zeros(carry.shape, F32)

    wrow = lax.broadcasted_iota(jnp.int32, (tm, tm), 0)
    trow = lax.broadcasted_iota(jnp.int32, (tm, tm), 1)
    select = (trow == (wrow % HALO_ROWS) * seg + wrow // HALO_ROWS).astype(BF16)
    up[...] = jnp.dot(select, u_ref[...].astype(BF16), preferred_element_type=F32).astype(BF16)

    sub = lax.broadcasted_iota(jnp.int32, (HALO_ROWS, ns), 0)
    zero = jnp.zeros((HALO_ROWS, ns), F32)
    n_buf = xs.shape[0]

    def input_states(cb):
        xs[cb % n_buf] = jnp.dot(up[:, cb * ch:(cb + 1) * ch], bw_ref[cb], preferred_element_type=F32)

    def output(cb):
        cx = _dot(xs[cb % n_buf], cw_ref[cb])
        for i in range(seg):
            y_s[cb, pl.ds(i, HALO_ROWS, stride=seg), :] = cx[i * HALO_ROWS:(i + 1) * HALO_ROWS, :]

    input_states(0)
    for cb in range(SSM_LANE_BLOCKS):
        if cb + 1 < SSM_LANE_BLOCKS:
            input_states(cb + 1)
        xb = xs.at[cb % n_buf]
        ar = a1_ref[cb, :, 0:ns]
        ai = a1_ref[cb, :, ns:2 * ns]
        er, ei = zero, zero
        for i in range(seg):
            rows = slice(i * HALO_ROWS, (i + 1) * HALO_ROWS)
            er, ei = (xb[rows, 0:ns] + (ar * er - ai * ei),
                      xb[rows, ns:2 * ns] + (ar * ei + ai * er))

        fr = jnp.where(sub == 0, carry[cb, :, 0:ns], pltpu.roll(er, 1, 0))
        fi = jnp.where(sub == 0, carry[cb, :, ns:2 * ns], pltpu.roll(ei, 1, 0))
        for k, shift in enumerate((1, 2, 4)):
            mr = hs_ref[k, cb, :, 0:ns]
            mi = hs_ref[k, cb, :, ns:2 * ns]
            pr = pltpu.roll(fr, shift, 0)
            pi = pltpu.roll(fi, shift, 0)
            fr, fi = fr + (mr * pr - mi * pi), fi + (mr * pi + mi * pr)
        lr = al_ref[cb, :, 0:ns]
        li = al_ref[cb, :, ns:2 * ns]
        last = slice(HALO_ROWS - 1, HALO_ROWS)
        carry[cb, :, 0:ns] = jnp.broadcast_to((lr * fr - li * fi + er)[last, :], (HALO_ROWS, ns))
        carry[cb, :, ns:2 * ns] = jnp.broadcast_to((lr * fi + li * fr + ei)[last, :], (HALO_ROWS, ns))

        for i in range(seg):
            rows = slice(i * HALO_ROWS, (i + 1) * HALO_ROWS)
            fr, fi = (xb[rows, 0:ns] + (ar * fr - ai * fi),
                      xb[rows, ns:2 * ns] + (ar * fi + ai * fr))
            xb[rows, 0:ns] = fr
            xb[rows, ns:2 * ns] = fi
        output(cb)

    y = jnp.concatenate([y_s[cb] for cb in range(SSM_LANE_BLOCKS)], axis=-1) + d_ref[...] * u_ref[...]
    y = 0.5 * y * (1.0 + jnp.tanh(math.sqrt(2.0 / math.pi) * (y + 0.044715 * (y * y * y))))
    glu = _dot(y, wg_ref[...]) + bg_ref[...]
    width = y.shape[1]
    o_ref[...] = (glu[:, :width] * _sigmoid(glu[:, width:]) * _silu(z_ref[...])).astype(o_ref.dtype)


def _s5(proj, log_dt, a_re, a_im, b_re, b_im, c_re, c_im, d_skip, w_glu, layer, b_glu, batch, seq, col, tm=512):
    tm = min(tm, seq)
    width = BRANCH_WIDTH
    nt = seq // tm
    seg = tm // HALO_ROWS
    nb = SSM_LANE_BLOCKS
    gb = SSM_GROUPS // nb
    ns = SSM_BLOCK_STATES
    bb_re, bb_im, p_re, p_im = _s5_discretize(log_dt, a_re, a_im, b_re, b_im, seg)
    eye = jnp.eye(gb, dtype=F32)

    def in_block(bb):
        bb = bb.reshape(nb, gb, SSM_GROUP, SSM_STATE)
        return jnp.einsum("bgcn,gh->bgchn", bb, eye).reshape(nb, gb * SSM_GROUP, ns)

    def out_block(c):
        c = c.reshape(nb, gb, SSM_GROUP, SSM_STATE)
        return jnp.einsum("bgcn,gh->bgnhc", c, eye).reshape(nb, ns, gb * SSM_GROUP)

    bw = jnp.concatenate([in_block(bb_re), in_block(bb_im)], axis=-1).astype(BF16)
    cw = jnp.concatenate([out_block(c_re), -out_block(c_im)], axis=1).astype(BF16)

    def lanes(p):
        return p.reshape(p.shape[0], nb, ns).transpose(1, 0, 2)

    sub = jnp.arange(HALO_ROWS)[:, None, None]

    def rows8(p_re_1, p_im_1, first_row=0):
        pr = jnp.where(sub >= first_row, p_re_1[None], 0.0)
        pi = jnp.where(sub >= first_row, p_im_1[None], 0.0)
        return jnp.concatenate([lanes(pr), lanes(pi)], axis=-1)

    a1 = rows8(p_re[0], p_im[0])
    al = rows8(p_re[seg - 1], p_im[seg - 1])
    hs = jnp.stack([rows8(p_re[seg - 1], p_im[seg - 1], 1), rows8(p_re[seg], p_im[seg], 2),
                    rows8(p_re[seg + 1], p_im[seg + 1], 4)])
    mult_spec = pl.BlockSpec((nb, HALO_ROWS, 2 * ns), lambda b, t: (0, 0, 0))
    return pl.pallas_call(
        _s5_kernel,
        out_shape=jax.ShapeDtypeStruct((batch * seq, width), BF16),
        grid=(batch, nt),
        in_specs=[
            pl.BlockSpec((tm, width), lambda b, t: (b * nt + t, col["su"] // width)),
            pl.BlockSpec((tm, width), lambda b, t: (b * nt + t, col["sz"] // width)),
            pl.BlockSpec((nb, gb * SSM_GROUP, 2 * ns), lambda b, t: (0, 0, 0)),
            pl.BlockSpec((nb, 2 * ns, gb * SSM_GROUP), lambda b, t: (0, 0, 0)),
            pl.BlockSpec((1, width), lambda b, t: (0, 0)),
            mult_spec,
            pl.BlockSpec((3, nb, HALO_ROWS, 2 * ns), lambda b, t: (0, 0, 0, 0)),
            mult_spec,
            pl.BlockSpec((None, width, 2 * width), lambda b, t: (layer, 0, 0)),
            pl.BlockSpec((1, 2 * width), lambda b, t: (0, 0)),
        ],
        out_specs=pl.BlockSpec((tm, width), lambda b, t: (b * nt + t, 0)),
        scratch_shapes=[pltpu.VMEM((tm, width), BF16),
                        pltpu.VMEM((2, tm, 2 * ns), F32),
                        pltpu.VMEM((nb, tm, gb * SSM_GROUP), F32),
                        pltpu.VMEM((nb, HALO_ROWS, 2 * ns), F32)],
        compiler_params=_params(("parallel", "arbitrary"), VMEM_LIMIT),
        name="s5",
    )(proj, proj, bw, cw, d_skip.reshape(1, width), a1, hs, al, w_glu, b_glu.reshape(1, 2 * width))


def _mem_attn_kernel(q_ref, z_ref, k_ref, v_ref, o_ref):
    dh = MEM_HEAD_DIM
    z = z_ref[...]
    for h in range(MEM_HEADS):
        cols = slice(h * dh, (h + 1) * dh)
        s = _dot_nt(q_ref[:, cols], k_ref[:, cols]) * (dh ** -0.5)
        s = s - jnp.max(s, axis=-1, keepdims=True)
        p = jnp.exp(s)
        p = p / jnp.sum(p, axis=-1, keepdims=True)
        o = _dot(p, v_ref[:, cols])
        o_ref[:, cols] = (o * _silu(z[:, cols])).astype(o_ref.dtype)


def _mem_attn(proj, kv, batch, seq, mem_len, col, tm=256):
    tm = min(tm, seq)
    width = BRANCH_WIDTH
    nt = seq // tm
    return pl.pallas_call(
        _mem_attn_kernel,
        out_shape=jax.ShapeDtypeStruct((batch * seq, width), BF16),
        grid=(batch, nt),
        in_specs=[
            pl.BlockSpec((tm, width), lambda b, t: (b * nt + t, col["mq"] // width)),
            pl.BlockSpec((tm, width), lambda b, t: (b * nt + t, col["mz"] // width)),
            pl.BlockSpec((mem_len, width), lambda b, t: (b, 0)),
            pl.BlockSpec((mem_len, width), lambda b, t: (b, 1)),
        ],
        out_specs=pl.BlockSpec((tm, width), lambda b, t: (b * nt + t, 0)),
        compiler_params=_params(("parallel", "parallel"), VMEM_LIMIT),
        name="mem_attn",
    )(proj, proj, kv, kv)


def _merge_kernel(g_ref, oa_ref, ob_ref, oc_ref, od_ref, wg_ref, bg_ref, wb_ref, o_ref):
    g_low = g_ref[...]
    acc = None
    for n, br_ref in enumerate((oa_ref, ob_ref, oc_ref, od_ref)):
        gate = _sigmoid(_dot(g_low, wg_ref[n]) + bg_ref[n:n + 1, :])
        term = gate * jnp.dot(br_ref[...], wb_ref[n], preferred_element_type=F32)
        acc = term if acc is None else acc + term
    o_ref[...] = acc.astype(o_ref.dtype)


def _merge(proj, branches, w_gate, layer, b_gate, w_branch, col, tm=1024, tn=512):
    m = proj.shape[0]
    d_model = w_gate.shape[-1]
    tm = min(tm, m)
    br_spec = pl.BlockSpec((tm, BRANCH_WIDTH), lambda i, j: (i, 0))
    return pl.pallas_call(
        _merge_kernel,
        out_shape=jax.ShapeDtypeStruct((m, d_model), BF16),
        grid=(m // tm, d_model // tn),
        in_specs=[
            pl.BlockSpec((tm, GATE_RANK), lambda i, j: (i, col["glow"] // GATE_RANK)),
            br_spec, br_spec, br_spec, br_spec,
            pl.BlockSpec((None, N_BRANCH, GATE_RANK, tn), lambda i, j: (layer, 0, 0, j)),
            pl.BlockSpec((N_BRANCH, tn), lambda i, j: (0, j)),
            pl.BlockSpec((None, N_BRANCH, BRANCH_WIDTH, tn), lambda i, j: (layer, 0, 0, j)),
        ],
        out_specs=pl.BlockSpec((tm, tn), lambda i, j: (i, j)),
        compiler_params=_params(("parallel", "arbitrary"), VMEM_LIMIT),
        name="merge",
    )(proj, *branches, w_gate, b_gate, w_branch)


RELAYOUT_COLS = 512


def _relayout_kernel(cur_ref, nxt_ref, small_ref, o_ref, prev, *, n_aligned, shift, tail):
    c = pl.program_id(2)
    last = pl.num_programs(2) - 1
    rows, tk = cur_ref.shape

    def joined(left):
        return jnp.concatenate([left[shift:, :], nxt_ref[0:shift, :]], axis=0)

    def emit(block):
        o_ref[...] = block.T.astype(o_ref.dtype)

    @pl.when(c < n_aligned)
    def _():
        emit(cur_ref[...])

    @pl.when(c == n_aligned)
    def _():
        emit(joined(cur_ref[...]))

    @pl.when((c > n_aligned) & (c < last))
    def _():
        emit(joined(prev[...]))

    @pl.when(c == last)
    def _():
        emit(jnp.concatenate([prev[shift:shift + tail, :], small_ref[0:shift, :],
                              jnp.zeros((rows - tail - shift, tk), F32)], axis=0))

    @pl.when(c >= n_aligned)
    def _():
        prev[...] = nxt_ref[...]


def _in_proj_layout(w_in, tk=2048):
    depth, d_model, d_in = w_in.shape
    cb = RELAYOUT_COLS
    a_end = 4 * BRANCH_WIDTH
    small = 2 * DN_HEADS
    main = 6 * BRANCH_WIDTH + GATE_RANK
    assert d_in == a_end + small + main and a_end % cb == 0
    n_aligned = a_end // cb
    n_out = -(-d_in // cb)
    tail = main - (n_out - 1 - n_aligned) * cb
    assert 0 < tail and tail + small <= cb
    n_in = -(-d_in // cb)
    tk = min(tk, d_model)
    w_t = jnp.transpose(w_in, (0, 2, 1))
    blk = lambda index: pl.BlockSpec((None, cb, tk), index)
    w = pl.pallas_call(
        functools.partial(_relayout_kernel, n_aligned=n_aligned, shift=small, tail=tail),
        out_shape=jax.ShapeDtypeStruct((depth, d_model, n_out * cb), BF16),
        grid=(depth, d_model // tk, n_out),
        in_specs=[blk(lambda l, k, c: (l, jnp.minimum(c, n_aligned), k)),
                  blk(lambda l, k, c: (l, jnp.clip(c + 1, n_aligned, n_in - 1), k)),
                  blk(lambda l, k, c: (l, n_aligned, k))],
        out_specs=pl.BlockSpec((None, tk, cb), lambda l, k, c: (l, k, c)),
        scratch_shapes=[pltpu.VMEM((cb, tk), F32)],
        compiler_params=_params(("parallel", "parallel", "arbitrary"), VMEM_LIMIT),
        name="in_proj_relayout",
    )(w_t, w_t, w_t)
    w1 = BRANCH_WIDTH
    col = {"qkv": 0, "dz": 3 * w1, "lx": 4 * w1, "lz": 5 * w1, "su": 6 * w1, "sz": 7 * w1,
           "mq": 8 * w1, "mz": 9 * w1, "glow": 10 * w1, "small": 10 * w1 + GATE_RANK}
    return w, col


def kernel(x, mem, norm_w, w_in, dn_conv_w, dn_a_log, dn_dt_bias, dn_norm_w, lru_conv_w, lru_conv_b, lru_w_r, lru_b_r, lru_w_i, lru_b_i, lru_lambda, ssm_log_dt, ssm_a_re, ssm_a_im, ssm_b_re, ssm_b_im, ssm_c_re, ssm_c_im, ssm_d, ssm_w_glu, ssm_b_glu, mem_norm_w, w_kv, w_gate, b_gate, w_branch, w_out, final_norm_w):
    batch, seq, d_model = x.shape
    mem_len = mem.shape[1]
    depth = w_in.shape[0]
    xf = x.reshape(batch * seq, d_model)
    memf = mem.reshape(batch * mem_len, d_model)
    w_cat, col = _in_proj_layout(w_in)
    w_gate_b = w_gate.astype(BF16)
    w_branch_b = w_branch.astype(BF16)
    w_out_b = w_out.astype(BF16)
    w_glu_b = ssm_w_glu.astype(BF16)
    w_ri = jnp.concatenate([lru_w_r, lru_w_i], axis=-1).astype(BF16)
    h = _rmsnorm(xf, norm_w[0], BF16)
    for l in range(depth):
        proj = _matmul(h, w_cat, l, F32)
        o_a = _deltanet(proj, dn_conv_w[l], dn_a_log[l], dn_dt_bias[l], dn_norm_w[l], batch, seq, col)
        o_b = _rglru(proj, lru_conv_w[l], lru_conv_b[l], w_ri, l, lru_b_r[l], lru_b_i[l],
                     lru_lambda[l], batch, seq, col)
        o_c = _s5(proj, ssm_log_dt[l], ssm_a_re[l], ssm_a_im[l], ssm_b_re[l], ssm_b_im[l],
                  ssm_c_re[l], ssm_c_im[l], ssm_d[l].reshape(-1), w_glu_b, l, ssm_b_glu[l], batch, seq, col)
        m_n = _rmsnorm(memf, mem_norm_w[l], BF16)
        kv = _matmul(m_n, w_kv, l, BF16)
        o_d = _mem_attn(proj, kv, batch, seq, mem_len, col)
        merged = _merge(proj, (o_a, o_b, o_c, o_d), w_gate_b, l, b_gate[l], w_branch_b, col)
        last = l == depth - 1
        next_norm_w = final_norm_w if last else norm_w[l + 1]
        xf, h = _out_proj_norm(merged, w_out_b, l, xf, next_norm_w, write_x=not last,
                               h_dtype=F32 if last else BF16)
    return h.reshape(batch, seq, d_model)
```

```python
import functools
import math

import jax
import jax.numpy as jnp
from jax import lax
from jax.experimental import pallas as pl
from jax.experimental.pallas import tpu as pltpu

F32 = jnp.float32
BF16 = jnp.bfloat16
HIGHEST = lax.Precision.HIGHEST

NORM_EPS = 1e-6
CONV_WIDTH = 4
HALO_ROWS = 8
N_BRANCH = 4
BRANCH_WIDTH = 1024
DN_HEAD_DIM = 128
DN_HEADS = 8
DN_CHUNK = 64
DN_TILE = 128
LRU_BLOCKS = 8
LRU_BLOCK = 128
LRU_C = 8.0
SSM_GROUP = 16
SSM_GROUPS = 64
SSM_STATE = 64
SSM_LANE_BLOCKS = 8
SSM_BLOCK_STATES = 512
MEM_HEADS = 4
MEM_HEAD_DIM = 256
GATE_RANK = 256
SMALL_COLS = 128

VMEM_LIMIT = 56 * 1024 * 1024


def _params(sem, vmem=None):
    return pltpu.CompilerParams(dimension_semantics=sem, vmem_limit_bytes=vmem)


def _sigmoid(x):
    return 0.5 * jnp.tanh(0.5 * x) + 0.5


def _silu(x):
    half = 0.5 * x
    return half * jnp.tanh(half) + half


def _softplus(x):
    return jnp.maximum(x, 0.0) + jnp.log1p(jnp.exp(-jnp.abs(x)))


def _dot(a, b):
    return jnp.dot(a.astype(BF16), b.astype(BF16), preferred_element_type=F32)


def _dot_nt(a, b):
    return lax.dot_general(a.astype(BF16), b.astype(BF16), (((1,), (1,)), ((), ())),
                           preferred_element_type=F32)


def _dot_f32(a, b):
    return jnp.dot(a, b, precision=HIGHEST, preferred_element_type=F32)


def _rmsnorm_kernel(x_ref, w_ref, o_ref):
    x = x_ref[...]
    var = jnp.mean(x * x, axis=-1, keepdims=True)
    o_ref[...] = (x * lax.rsqrt(var + NORM_EPS) * w_ref[...]).astype(o_ref.dtype)


def _rmsnorm(x, w, out_dtype, tm=512):
    m, d = x.shape
    tm = min(tm, m)
    return pl.pallas_call(
        _rmsnorm_kernel,
        out_shape=jax.ShapeDtypeStruct((m, d), out_dtype),
        grid=(m // tm,),
        in_specs=[pl.BlockSpec((tm, d), lambda i: (i, 0)),
                  pl.BlockSpec((1, d), lambda i: (0, 0))],
        out_specs=pl.BlockSpec((tm, d), lambda i: (i, 0)),
        compiler_params=_params(("parallel",)),
        name="rmsnorm",
    )(x, w.reshape(1, d))


def _mm_kernel(a_ref, w_ref, o_ref):
    o_ref[...] = jnp.dot(a_ref[...], w_ref[...].astype(BF16), preferred_element_type=F32).astype(o_ref.dtype)


def _matmul(a, w, layer, out_dtype, tm=1024, tn=512):
    m, k = a.shape
    n = w.shape[2]
    tm = min(tm, m)
    tn = min(tn, n)
    return pl.pallas_call(
        _mm_kernel,
        out_shape=jax.ShapeDtypeStruct((m, n), out_dtype),
        grid=(m // tm, n // tn),
        in_specs=[pl.BlockSpec((tm, k), lambda i, j: (i, 0)),
                  pl.BlockSpec((None, k, tn), lambda i, j: (layer, 0, j))],
        out_specs=pl.BlockSpec((tm, tn), lambda i, j: (i, j)),
        compiler_params=_params(("parallel", "arbitrary"), VMEM_LIMIT),
        name="matmul",
    )(a, w)


def _out_norm_kernel(a_ref, w_ref, r_ref, nw_ref, *refs, write_x):
    if write_x:
        x_ref, h_ref, acc = refs
    else:
        h_ref, acc = refs
    j = pl.program_id(1)
    nj, _, tn = acc.shape
    acc[j] = r_ref[...] + jnp.dot(a_ref[...], w_ref[...], preferred_element_type=F32)

    @pl.when(j == nj - 1)
    def _():
        ssq = None
        for jj in range(nj):
            blk = acc[jj]
            part = jnp.sum(blk * blk, axis=-1, keepdims=True)
            ssq = part if ssq is None else ssq + part
        inv = lax.rsqrt(ssq * (1.0 / (nj * tn)) + NORM_EPS)
        for jj in range(nj):
            blk = acc[jj]
            cols = slice(jj * tn, (jj + 1) * tn)
            if write_x:
                x_ref[:, cols] = blk
            h_ref[:, cols] = (blk * inv * nw_ref[:, cols]).astype(h_ref.dtype)


def _out_proj_norm(a, w, layer, residual, norm_w, write_x, h_dtype, tm=512, tn=512):
    m, k = a.shape
    n = w.shape[2]
    tm = min(tm, m)
    row_spec = pl.BlockSpec((tm, n), lambda i, j: (i, 0))
    out_shape = [jax.ShapeDtypeStruct((m, n), h_dtype)]
    out_specs = [row_spec]
    if write_x:
        out_shape.insert(0, jax.ShapeDtypeStruct((m, n), F32))
        out_specs.insert(0, row_spec)
    outs = pl.pallas_call(
        functools.partial(_out_norm_kernel, write_x=write_x),
        out_shape=out_shape,
        grid=(m // tm, n // tn),
        in_specs=[pl.BlockSpec((tm, k), lambda i, j: (i, 0)),
                  pl.BlockSpec((None, k, tn), lambda i, j: (layer, 0, j)),
                  pl.BlockSpec((tm, tn), lambda i, j: (i, j)),
                  pl.BlockSpec((1, n), lambda i, j: (0, 0))],
        out_specs=out_specs,
        scratch_shapes=[pltpu.VMEM((n // tn, tm, tn), F32)],
        compiler_params=_params(("parallel", "arbitrary"), VMEM_LIMIT),
        name="out_proj_norm",
    )(a, w, residual, norm_w.reshape(1, n))
    return (outs[0], outs[1]) if write_x else (None, outs[0])


def _fill_conv_buffer(buf, halo_ref, x_ref, t, tm):
    @pl.when(t == 0)
    def _():
        buf[0:HALO_ROWS, :] = jnp.zeros((HALO_ROWS, buf.shape[1]), F32)

    @pl.when(t != 0)
    def _():
        buf[0:HALO_ROWS, :] = halo_ref[...]

    buf[HALO_ROWS:HALO_ROWS + tm, :] = x_ref[...]


def _causal_conv(buf, cw_ref, tm):
    assert CONV_WIDTH == 4
    x = buf[...]
    x1 = pltpu.roll(x, 1, 0)
    early = cw_ref[0:1, :] * x1 + cw_ref[1:2, :] * x
    late = cw_ref[2:3, :] * x1 + cw_ref[3:4, :] * x
    return (pltpu.roll(early, 2, 0) + late)[HALO_ROWS:HALO_ROWS + tm, :]


def _halo_index(seq_blocks, tile_blocks):
    def index(b, t, col):
        return (jnp.maximum(b * seq_blocks + t * tile_blocks - 1, 0), col)
    return index


def _dn_kernel(qkv_ref, halo_ref, z_ref, sm_ref, cw_ref, alog_ref, dtb_ref, nw_ref, o_ref,
               buf, state):
    t = pl.program_id(1)
    slots, tm, _ = qkv_ref.shape
    d = DN_HEAD_DIM

    @pl.when(t == 0)
    def _():
        state[...] = jnp.zeros(state.shape, F32)

    for slot in range(slots):
        _fill_conv_buffer(buf.at[slot], halo_ref.at[slot], qkv_ref.at[slot], t, tm)

    row = lax.broadcasted_iota(jnp.int32, (tm, tm), 0)
    col = lax.broadcasted_iota(jnp.int32, (tm, tm), 1)
    same_chunk = (row // DN_CHUNK) == (col // DN_CHUNK)
    causal = same_chunk & (row >= col)
    strict = same_chunk & (row > col)
    causal_f = causal.astype(F32)
    same_chunk_f = same_chunk.astype(F32)

    heads = range(DN_HEADS)
    chunks = range(tm // DN_CHUNK)
    zero_chunk = jnp.zeros((DN_CHUNK, 2 * d), F32)
    nw = nw_ref[...]

    def prepare(slot):
        qkv = _silu(_causal_conv(buf.at[slot], cw_ref, tm))
        small = sm_ref[slot]
        g_all = -jnp.exp(alog_ref[...]) * _softplus(small + dtb_ref[...])
        gc_all = _dot_f32(causal_f, g_all)
        return dict(qkv=qkv, beta=_sigmoid(small), gc=gc_all,
                    gl=_dot_f32(same_chunk_f, g_all),
                    gc_rows=gc_all.T)

    def head_inputs(ctx, h):
        qkv = ctx["qkv"]
        qh = qkv[:, h * d:(h + 1) * d]
        kh = qkv[:, BRANCH_WIDTH + h * d:BRANCH_WIDTH + (h + 1) * d]
        vh = qkv[:, 2 * BRANCH_WIDTH + h * d:2 * BRANCH_WIDTH + (h + 1) * d]
        qh = qh * (lax.rsqrt(jnp.sum(qh * qh, axis=-1, keepdims=True) + NORM_EPS) * (d ** -0.5))
        kh = kh * lax.rsqrt(jnp.sum(kh * kh, axis=-1, keepdims=True) + NORM_EPS)
        beta = ctx["beta"][:, h:h + 1]
        gc = ctx["gc"][:, DN_HEADS + h:DN_HEADS + h + 1]
        gl = ctx["gl"][:, DN_HEADS + h:DN_HEADS + h + 1]
        gr = ctx["gc_rows"][DN_HEADS + h:DN_HEADS + h + 1, :]
        eg = jnp.exp(gc)
        kbh = kh * beta
        return dict(
            q=qh.astype(BF16), k=kh.astype(BF16), kb=kbh.astype(BF16), qd=qh * eg,
            kd_t=(kh * jnp.exp(gl - gc)).T.astype(BF16),
            decay=jnp.where(causal, jnp.exp(jnp.where(causal, gc - gr, 0.0)), 0.0),
            sol=jnp.concatenate([vh * beta, kbh * eg], axis=-1),
            gl=gl)

    def solve_and_scan(slot, hd, between):
        x, qk = [], []
        for h in heads:
            both = lax.dot_general(jnp.concatenate([hd[h]["kb"], hd[h]["q"]], axis=0), hd[h]["k"],
                                   (((1,), (1,)), ((), ())), preferred_element_type=F32)
            x.append(jnp.where(strict, -(both[:tm] * hd[h]["decay"]), 0.0))
            qk.append(jnp.where(causal, both[tm:] * hd[h]["decay"], 0.0).astype(BF16))
        between()

        sol = [hd[h]["sol"] for h in heads]
        n_square = 5
        for it in range(n_square + 1):
            for h in heads:
                xb = x[h].astype(BF16)
                sb = sol[h].astype(BF16)
                if it < n_square:
                    y = jnp.dot(xb, jnp.concatenate([xb, sb], axis=-1), preferred_element_type=F32)
                    x[h] = y[:, :tm]
                    sol[h] = sol[h] + y[:, tm:]
                else:
                    sol[h] = sol[h] + jnp.dot(xb, sb, preferred_element_type=F32)
            between()

        kw, ku = [], []
        for h in heads:
            kw_h, ku_h = [], []
            for c in chunks:
                rows = slice(c * DN_CHUNK, (c + 1) * DN_CHUNK)
                wu = jnp.concatenate([sol[h][rows, d:], sol[h][rows, :d]], axis=-1)
                padded = jnp.concatenate([wu if i == c else zero_chunk for i in chunks], axis=0)
                prod = jnp.dot(hd[h]["kd_t"], padded.astype(BF16), preferred_element_type=F32)
                kw_h.append(prod[:, :d].astype(BF16))
                ku_h.append(prod[:, d:])
            kw.append(kw_h)
            ku.append(ku_h)
        between()

        s = [state[slot, h] for h in heads]
        inter = [[] for _ in heads]
        v_new = [[] for _ in heads]
        for c in chunks:
            rows = slice(c * DN_CHUNK, (c + 1) * DN_CHUNK)
            for h in heads:
                sb = s[h].astype(BF16)
                lhs = jnp.concatenate([hd[h]["qd"][rows], sol[h][rows, d:]], axis=0).astype(BF16)
                r = jnp.dot(lhs, sb, preferred_element_type=F32)
                inter[h].append(r[:DN_CHUNK])
                v_new[h].append(sol[h][rows, :d] - r[DN_CHUNK:])
                s[h] = (s[h] * jnp.exp(hd[h]["gl"][c * DN_CHUNK:c * DN_CHUNK + 1, :])
                        - jnp.dot(kw[h][c], sb, preferred_element_type=F32) + ku[h][c])
            between()
        z = z_ref[slot]
        for h in heads:
            state[slot, h] = s[h]
            out = jnp.concatenate(inter[h], axis=0) + jnp.dot(
                qk[h], jnp.concatenate(v_new[h], axis=0).astype(BF16), preferred_element_type=F32)
            var = jnp.mean(out * out, axis=-1, keepdims=True)
            out = out * lax.rsqrt(var + NORM_EPS) * nw
            o_ref[slot, :, h * d:(h + 1) * d] = (out * _silu(z[:, h * d:(h + 1) * d])).astype(o_ref.dtype)

    ctx = prepare(0)
    inputs = [head_inputs(ctx, h) for h in heads]
    for slot in range(slots):
        following = []
        pieces = []
        if slot + 1 < slots:
            pieces = [lambda nxt=slot + 1: following.append(prepare(nxt))] + [
                (lambda h=h: following.append(head_inputs(following[0], h))) for h in heads]

        def between(pieces=pieces):
            if pieces:
                pieces.pop(0)()

        solve_and_scan(slot, inputs, between)
        while pieces:
            pieces.pop(0)()
        inputs = following[1:]


def _deltanet(proj, conv_w, a_log, dt_bias, norm_w, batch, seq, col):
    tm = DN_TILE
    width = 3 * BRANCH_WIDTH
    slots = 2 if batch % 2 == 0 else 1
    pad = (0, SMALL_COLS - 2 * DN_HEADS)
    alog = jnp.pad(jnp.concatenate([jnp.zeros((DN_HEADS,), F32), a_log]), pad).reshape(1, SMALL_COLS)
    dtb = jnp.pad(jnp.concatenate([jnp.zeros((DN_HEADS,), F32), dt_bias]), pad).reshape(1, SMALL_COLS)
    nt = seq // tm
    proj3 = proj.reshape(batch, seq, proj.shape[1])
    halo_blocks = tm // HALO_ROWS
    out = pl.pallas_call(
        _dn_kernel,
        out_shape=jax.ShapeDtypeStruct((batch, seq, BRANCH_WIDTH), BF16),
        grid=(batch // slots, nt),
        in_specs=[
            pl.BlockSpec((slots, tm, width), lambda b, t: (b, t, col["qkv"] // width)),
            pl.BlockSpec((slots, HALO_ROWS, width),
                         lambda b, t: (b, jnp.maximum(t * halo_blocks - 1, 0), col["qkv"] // width)),
            pl.BlockSpec((slots, tm, BRANCH_WIDTH), lambda b, t: (b, t, col["dz"] // BRANCH_WIDTH)),
            pl.BlockSpec((slots, tm, SMALL_COLS), lambda b, t: (b, t, col["small"] // SMALL_COLS)),
            pl.BlockSpec((CONV_WIDTH, width), lambda b, t: (0, 0)),
            pl.BlockSpec((1, SMALL_COLS), lambda b, t: (0, 0)),
            pl.BlockSpec((1, SMALL_COLS), lambda b, t: (0, 0)),
            pl.BlockSpec((1, DN_HEAD_DIM), lambda b, t: (0, 0)),
        ],
        out_specs=pl.BlockSpec((slots, tm, BRANCH_WIDTH), lambda b, t: (b, t, 0)),
        scratch_shapes=[pltpu.VMEM((slots, HALO_ROWS + tm, width), F32),
                        pltpu.VMEM((slots, DN_HEADS, DN_HEAD_DIM, DN_HEAD_DIM), F32)],
        compiler_params=_params(("parallel", "arbitrary"), VMEM_LIMIT),
        name="deltanet",
    )(proj3, proj3, proj3, proj3, conv_w, alog, dtb, norm_w.reshape(1, DN_HEAD_DIM))
    return out.reshape(batch * seq, BRANCH_WIDTH)


def _lru_kernel(x_ref, halo_ref, z_ref, cw_ref, cb_ref, w_ref, b_ref, lam_ref, o_ref,
                buf, a_s, b_s, carry):
    t = pl.program_id(1)
    tm = x_ref.shape[0]
    width = x_ref.shape[1]

    @pl.when(t == 0)
    def _():
        carry[...] = jnp.zeros(carry.shape, F32)

    _fill_conv_buffer(buf, halo_ref, x_ref, t, tm)
    xc = _causal_conv(buf, cw_ref, tm) + cb_ref[...]
    gates = []
    for n in range(LRU_BLOCKS):
        gates.append(_dot(xc[:, n * LRU_BLOCK:(n + 1) * LRU_BLOCK], w_ref[n]))
    r = jnp.concatenate([g[:, :LRU_BLOCK] for g in gates], axis=-1)
    i = jnp.concatenate([g[:, LRU_BLOCK:] for g in gates], axis=-1)
    r = _sigmoid(r + b_ref[0:1, :])
    i = _sigmoid(i + b_ref[1:2, :])
    log_a = -LRU_C * r * _softplus(-lam_ref[...])
    a = jnp.exp(log_a)
    a_s[...] = a
    one_minus_a2 = 1.0 - a * a
    root = jnp.where(one_minus_a2 > 0.0, one_minus_a2 * lax.rsqrt(one_minus_a2), 0.0)
    b_s[...] = root * (i * xc)

    sub = lax.broadcasted_iota(jnp.int32, (HALO_ROWS, width), 0)

    def group(g, h_prev):
        rows = pl.ds(pl.multiple_of(g * HALO_ROWS, HALO_ROWS), HALO_ROWS)
        a = a_s[rows, :]
        b = b_s[rows, :]
        for shift in (1, 2, 4):
            keep = sub >= shift
            a_prev = jnp.where(keep, pltpu.roll(a, shift, 0), 1.0)
            b_prev = jnp.where(keep, pltpu.roll(b, shift, 0), 0.0)
            b = a * b_prev + b
            a = a * a_prev
        h = a * h_prev + b
        b_s[rows, :] = h
        return jnp.broadcast_to(h[HALO_ROWS - 1:HALO_ROWS, :], (HALO_ROWS, width))

    carry[...] = lax.fori_loop(0, tm // HALO_ROWS, group, carry[...])
    o_ref[...] = (b_s[...] * _silu(z_ref[...])).astype(o_ref.dtype)


def _rglru(proj, conv_w, conv_b, w_ri, layer, b_r, b_i, lam, batch, seq, col, tm=256):
    tm = min(tm, seq)
    width = BRANCH_WIDTH
    nt = seq // tm
    b_ri = jnp.stack([b_r, b_i])
    return pl.pallas_call(
        _lru_kernel,
        out_shape=jax.ShapeDtypeStruct((batch * seq, width), BF16),
        grid=(batch, nt),
        in_specs=[
            pl.BlockSpec((tm, width), lambda b, t: (b * nt + t, col["lx"] // width)),
            pl.BlockSpec((HALO_ROWS, width),
                         functools.partial(_halo_index(seq // HALO_ROWS, tm // HALO_ROWS),
                                           col=col["lx"] // width)),
            pl.BlockSpec((tm, width), lambda b, t: (b * nt + t, col["lz"] // width)),
            pl.BlockSpec((CONV_WIDTH, width), lambda b, t: (0, 0)),
            pl.BlockSpec((1, width), lambda b, t: (0, 0)),
            pl.BlockSpec((None, LRU_BLOCKS, LRU_BLOCK, 2 * LRU_BLOCK), lambda b, t: (layer, 0, 0, 0)),
            pl.BlockSpec((2, width), lambda b, t: (0, 0)),
            pl.BlockSpec((1, width), lambda b, t: (0, 0)),
        ],
        out_specs=pl.BlockSpec((tm, width), lambda b, t: (b * nt + t, 0)),
        scratch_shapes=[pltpu.VMEM((HALO_ROWS + tm, width), F32),
                        pltpu.VMEM((tm, width), F32),
                        pltpu.VMEM((tm, width), F32),
                        pltpu.VMEM((HALO_ROWS, width), F32)],
        compiler_params=_params(("parallel", "arbitrary"), VMEM_LIMIT),
        name="rglru",
    )(proj, proj, proj, conv_w, conv_b.reshape(1, width), w_ri, b_ri, lam.reshape(1, width))


def _zoh(ldt, a_re, a_im):
    dt = jnp.exp(ldt)
    mag = jnp.exp(dt * a_re)
    return mag * jnp.cos(dt * a_im), mag * jnp.sin(dt * a_im)


def _s5_discretize_kernel(ldt_ref, are_ref, aim_ref, bre_ref, bim_ref, ldtg_ref, areg_ref, aimg_ref,
                          bbre_ref, bbim_ref, pre_ref, pim_ref):
    a_re = are_ref[...]
    a_im = aim_ref[...]
    ab_re, ab_im = _zoh(ldt_ref[...], a_re, a_im)
    den = a_re * a_re + a_im * a_im
    f_re = ((ab_re - 1.0) * a_re + ab_im * a_im) / den
    f_im = (ab_im * a_re - (ab_re - 1.0) * a_im) / den
    b_re = bre_ref[...]
    b_im = bim_ref[...]
    bbre_ref[...] = f_re * b_re - f_im * b_im
    bbim_ref[...] = f_re * b_im + f_im * b_re

    seg_len = pre_ref.shape[0] - 2
    ab_re, ab_im = _zoh(ldtg_ref[...], areg_ref[...], aimg_ref[...])
    p_re, p_im = ab_re, ab_im
    pre_ref[0] = p_re
    pim_ref[0] = p_im
    for p in range(1, seg_len):
        p_re, p_im = p_re * ab_re - p_im * ab_im, p_re * ab_im + p_im * ab_re
        pre_ref[p] = p_re
        pim_ref[p] = p_im
    for p in range(seg_len, seg_len + 2):
        p_re, p_im = p_re * p_re - p_im * p_im, 2.0 * (p_re * p_im)
        pre_ref[p] = p_re
        pim_ref[p] = p_im


def _s5_discretize(log_dt, a_re, a_im, b_re, b_im, seg_len):
    rows = SSM_GROUPS * SSM_GROUP
    rep = lambda p: jnp.repeat(p, SSM_GROUP, axis=0)
    ldt = jnp.broadcast_to(log_dt[:, None], (SSM_GROUPS, SSM_STATE))
    to_rows = lambda b: b.transpose(0, 2, 1).reshape(rows, SSM_STATE)
    shp = jax.ShapeDtypeStruct((rows, SSM_STATE), F32)
    pshp = jax.ShapeDtypeStruct((seg_len + 2, SSM_GROUPS, SSM_STATE), F32)
    return pl.pallas_call(
        _s5_discretize_kernel, out_shape=(shp, shp, pshp, pshp), name="s5_discretize",
    )(rep(ldt), rep(a_re), rep(a_im), to_rows(b_re), to_rows(b_im), ldt, a_re, a_im)


def _s5_kernel(u_ref, z_ref, bw_ref, cw_ref, d_ref, a1_ref, hs_ref, al_ref,
               wg_ref, bg_ref, o_ref, up, xs, y_s, carry):
    t = pl.program_id(1)
    tm = u_ref.shape[0]
    seg = tm // HALO_ROWS
    ns = SSM_BLOCK_STATES
    ch = SSM_GROUP * (SSM_GROUPS // SSM_LANE_BLOCKS)

    @pl.when(t == 0)
    def _():
        carry[...] = jnp.zeros(carry.shape, F32)

    wrow = lax.broadcasted_iota(jnp.int32, (tm, tm), 0)
    trow = lax.broadcasted_iota(jnp.int32, (tm, tm), 1)
    select = (trow == (wrow % HALO_ROWS) * seg + wrow // HALO_ROWS).astype(BF16)
    up[...] = jnp.dot(select, u_ref[...].astype(BF16), preferred_element_type=F32).astype(BF16)

    sub = lax.broadcasted_iota(jnp.int32, (HALO_ROWS, ns), 0)
    zero = jnp.zeros((HALO_ROWS, ns), F32)
    n_buf = xs.shape[0]

    def input_states(cb):
        xs[cb % n_buf] = jnp.dot(up[:, cb * ch:(cb + 1) * ch], bw_ref[cb], preferred_element_type=F32)

    def output(cb):
        cx = _dot(xs[cb % n_buf], cw_ref[cb])
        for i in range(seg):
            y_s[cb, pl.ds(i, HALO_ROWS, stride=seg), :] = cx[i * HALO_ROWS:(i + 1) * HALO_ROWS, :]

    input_states(0)
    for cb in range(SSM_LANE_BLOCKS):
        if cb + 1 < SSM_LANE_BLOCKS:
            input_states(cb + 1)
        xb = xs.at[cb % n_buf]
        ar = a1_ref[cb, :, 0:ns]
        ai = a1_ref[cb, :, ns:2 * ns]
        er, ei = zero, zero
        for i in range(seg):
            rows = slice(i * HALO_ROWS, (i + 1) * HALO_ROWS)
            er, ei = (xb[rows, 0:ns] + (ar * er - ai * ei),
                      xb[rows, ns:2 * ns] + (ar * ei + ai * er))

        fr = jnp.where(sub == 0, carry[cb, :, 0:ns], pltpu.roll(er, 1, 0))
        fi = jnp.where(sub == 0, carry[cb, :, ns:2 * ns], pltpu.roll(ei, 1, 0))
        for k, shift in enumerate((1, 2, 4)):
            mr = hs_ref[k, cb, :, 0:ns]
            mi = hs_ref[k, cb, :, ns:2 * ns]
            pr = pltpu.roll(fr, shift, 0)
            pi = pltpu.roll(fi, shift, 0)
            fr, fi = fr + (mr * pr - mi * pi), fi + (mr * pi + mi * pr)
        lr = al_ref[cb, :, 0:ns]
        li = al_ref[cb, :, ns:2 * ns]
        last = slice(HALO_ROWS - 1, HALO_ROWS)
        carry[cb, :, 0:ns] = jnp.broadcast_to((lr * fr - li * fi + er)[last, :], (HALO_ROWS, ns))
        carry[cb, :, ns:2 * ns] = jnp.broadcast_to((lr * fi + li * fr + ei)[last, :], (HALO_ROWS, ns))

        for i in range(seg):
            rows = slice(i * HALO_ROWS, (i + 1) * HALO_ROWS)
            fr, fi = (xb[rows, 0:ns] + (ar * fr - ai * fi),
                      xb[rows, ns:2 * ns] + (ar * fi + ai * fr))
            xb[rows, 0:ns] = fr
            xb[rows, ns:2 * ns] = fi
        output(cb)

    y = jnp.concatenate([y_s[cb] for cb in range(SSM_LANE_BLOCKS)], axis=-1) + d_ref[...] * u_ref[...]
    y = 0.5 * y * (1.0 + jnp.tanh(math.sqrt(2.0 / math.pi) * (y + 0.044715 * (y * y * y))))
    glu = _dot(y, wg_ref[...]) + bg_ref[...]
    width = y.shape[1]
    o_ref[...] = (glu[:, :width] * _sigmoid(glu[:, width:]) * _silu(z_ref[...])).astype(o_ref.dtype)


def _s5(proj, log_dt, a_re, a_im, b_re, b_im, c_re, c_im, d_skip, w_glu, layer, b_glu, batch, seq, col, tm=512):
    tm = min(tm, seq)
    width = BRANCH_WIDTH
    nt = seq // tm
    seg = tm // HALO_ROWS
    nb = SSM_LANE_BLOCKS
    gb = SSM_GROUPS // nb
    ns = SSM_BLOCK_STATES
    bb_re, bb_im, p_re, p_im = _s5_discretize(log_dt, a_re, a_im, b_re, b_im, seg)
    eye = jnp.eye(gb, dtype=F32)

    def in_block(bb):
        bb = bb.reshape(nb, gb, SSM_GROUP, SSM_STATE)
        return jnp.einsum("bgcn,gh->bgchn", bb, eye).reshape(nb, gb * SSM_GROUP, ns)

    def out_block(c):
        c = c.reshape(nb, gb, SSM_GROUP, SSM_STATE)
        return jnp.einsum("bgcn,gh->bgnhc", c, eye).reshape(nb, ns, gb * SSM_GROUP)

    bw = jnp.concatenate([in_block(bb_re), in_block(bb_im)], axis=-1).astype(BF16)
    cw = jnp.concatenate([out_block(c_re), -out_block(c_im)], axis=1).astype(BF16)

    def lanes(p):
        return p.reshape(p.shape[0], nb, ns).transpose(1, 0, 2)

    sub = jnp.arange(HALO_ROWS)[:, None, None]

    def rows8(p_re_1, p_im_1, first_row=0):
        pr = jnp.where(sub >= first_row, p_re_1[None], 0.0)
        pi = jnp.where(sub >= first_row, p_im_1[None], 0.0)
        return jnp.concatenate([lanes(pr), lanes(pi)], axis=-1)

    a1 = rows8(p_re[0], p_im[0])
    al = rows8(p_re[seg - 1], p_im[seg - 1])
    hs = jnp.stack([rows8(p_re[seg - 1], p_im[seg - 1], 1), rows8(p_re[seg], p_im[seg], 2),
                    rows8(p_re[seg + 1], p_im[seg + 1], 4)])
    mult_spec = pl.BlockSpec((nb, HALO_ROWS, 2 * ns), lambda b, t: (0, 0, 0))
    return pl.pallas_call(
        _s5_kernel,
        out_shape=jax.ShapeDtypeStruct((batch * seq, width), BF16),
        grid=(batch, nt),
        in_specs=[
            pl.BlockSpec((tm, width), lambda b, t: (b * nt + t, col["su"] // width)),
            pl.BlockSpec((tm, width), lambda b, t: (b * nt + t, col["sz"] // width)),
            pl.BlockSpec((nb, gb * SSM_GROUP, 2 * ns), lambda b, t: (0, 0, 0)),
            pl.BlockSpec((nb, 2 * ns, gb * SSM_GROUP), lambda b, t: (0, 0, 0)),
            pl.BlockSpec((1, width), lambda b, t: (0, 0)),
            mult_spec,
            pl.BlockSpec((3, nb, HALO_ROWS, 2 * ns), lambda b, t: (0, 0, 0, 0)),
            mult_spec,
            pl.BlockSpec((None, width, 2 * width), lambda b, t: (layer, 0, 0)),
            pl.BlockSpec((1, 2 * width), lambda b, t: (0, 0)),
        ],
        out_specs=pl.BlockSpec((tm, width), lambda b, t: (b * nt + t, 0)),
        scratch_shapes=[pltpu.VMEM((tm, width), BF16),
                        pltpu.VMEM((2, tm, 2 * ns), F32),
                        pltpu.VMEM((nb, tm, gb * SSM_GROUP), F32),
                        pltpu.VMEM((nb, HALO_ROWS, 2 * ns), F32)],
        compiler_params=_params(("parallel", "arbitrary"), VMEM_LIMIT),
        name="s5",
    )(proj, proj, bw, cw, d_skip.reshape(1, width), a1, hs, al, w_glu, b_glu.reshape(1, 2 * width))


def _mem_attn_kernel(q_ref, z_ref, k_ref, v_ref, o_ref):
    dh = MEM_HEAD_DIM
    z = z_ref[...]
    for h in range(MEM_HEADS):
        cols = slice(h * dh, (h + 1) * dh)
        s = _dot_nt(q_ref[:, cols], k_ref[:, cols]) * (dh ** -0.5)
        s = s - jnp.max(s, axis=-1, keepdims=True)
        p = jnp.exp(s)
        p = p / jnp.sum(p, axis=-1, keepdims=True)
        o = _dot(p, v_ref[:, cols])
        o_ref[:, cols] = (o * _silu(z[:, cols])).astype(o_ref.dtype)


def _mem_attn(proj, kv, batch, seq, mem_len, col, tm=1024):
    tm = min(tm, seq)
    width = BRANCH_WIDTH
    nt = seq // tm
    return pl.pallas_call(
        _mem_attn_kernel,
        out_shape=jax.ShapeDtypeStruct((batch * seq, width), BF16),
        grid=(batch, nt),
        in_specs=[
            pl.BlockSpec((tm, width), lambda b, t: (b * nt + t, col["mq"] // width)),
            pl.BlockSpec((tm, width), lambda b, t: (b * nt + t, col["mz"] // width)),
            pl.BlockSpec((mem_len, width), lambda b, t: (b, 0)),
            pl.BlockSpec((mem_len, width), lambda b, t: (b, 1)),
        ],
        out_specs=pl.BlockSpec((tm, width), lambda b, t: (b * nt + t, 0)),
        compiler_params=_params(("parallel", "parallel"), VMEM_LIMIT),
        name="mem_attn",
    )(proj, proj, kv, kv)


def _merge_kernel(g_ref, oa_ref, ob_ref, oc_ref, od_ref, wg_ref, bg_ref, wb_ref, o_ref):
    g_low = g_ref[...]
    acc = None
    for n, br_ref in enumerate((oa_ref, ob_ref, oc_ref, od_ref)):
        gate = _sigmoid(_dot(g_low, wg_ref[n]) + bg_ref[n:n + 1, :])
        term = gate * jnp.dot(br_ref[...], wb_ref[n], preferred_element_type=F32)
        acc = term if acc is None else acc + term
    o_ref[...] = acc.astype(o_ref.dtype)


def _merge(proj, branches, w_gate, layer, b_gate, w_branch, col, tm=1024, tn=512):
    m = proj.shape[0]
    d_model = w_gate.shape[-1]
    tm = min(tm, m)
    br_spec = pl.BlockSpec((tm, BRANCH_WIDTH), lambda i, j: (i, 0))
    return pl.pallas_call(
        _merge_kernel,
        out_shape=jax.ShapeDtypeStruct((m, d_model), BF16),
        grid=(m // tm, d_model // tn),
        in_specs=[
            pl.BlockSpec((tm, GATE_RANK), lambda i, j: (i, col["glow"] // GATE_RANK)),
            br_spec, br_spec, br_spec, br_spec,
            pl.BlockSpec((None, N_BRANCH, GATE_RANK, tn), lambda i, j: (layer, 0, 0, j)),
            pl.BlockSpec((N_BRANCH, tn), lambda i, j: (0, j)),
            pl.BlockSpec((None, N_BRANCH, BRANCH_WIDTH, tn), lambda i, j: (layer, 0, 0, j)),
        ],
        out_specs=pl.BlockSpec((tm, tn), lambda i, j: (i, j)),
        compiler_params=_params(("parallel", "arbitrary"), VMEM_LIMIT),
        name="merge",
    )(proj, *branches, w_gate, b_gate, w_branch)


RELAYOUT_COLS = 512


def _relayout_kernel(cur_ref, nxt_ref, small_ref, o_ref, prev, *, n_aligned, shift, tail):
    c = pl.program_id(2)
    last = pl.num_programs(2) - 1
    rows, tk = cur_ref.shape

    def joined(left):
        return jnp.concatenate([left[shift:, :], nxt_ref[0:shift, :]], axis=0)

    def emit(block):
        o_ref[...] = block.T.astype(o_ref.dtype)

    @pl.when(c < n_aligned)
    def _():
        emit(cur_ref[...])

    @pl.when(c == n_aligned)
    def _():
        emit(joined(cur_ref[...]))

    @pl.when((c > n_aligned) & (c < last))
    def _():
        emit(joined(prev[...]))

    @pl.when(c == last)
    def _():
        emit(jnp.concatenate([prev[shift:shift + tail, :], small_ref[0:shift, :],
                              jnp.zeros((rows - tail - shift, tk), F32)], axis=0))

    @pl.when(c >= n_aligned)
    def _():
        prev[...] = nxt_ref[...]


def _in_proj_layout(w_in, tk=2048):
    depth, d_model, d_in = w_in.shape
    cb = RELAYOUT_COLS
    a_end = 4 * BRANCH_WIDTH
    small = 2 * DN_HEADS
    main = 6 * BRANCH_WIDTH + GATE_RANK
    assert d_in == a_end + small + main and a_end % cb == 0
    n_aligned = a_end // cb
    n_out = -(-d_in // cb)
    tail = main - (n_out - 1 - n_aligned) * cb
    assert 0 < tail and tail + small <= cb
    n_in = -(-d_in // cb)
    tk = min(tk, d_model)
    w_t = jnp.transpose(w_in, (0, 2, 1))
    blk = lambda index: pl.BlockSpec((None, cb, tk), index)
    w = pl.pallas_call(
        functools.partial(_relayout_kernel, n_aligned=n_aligned, shift=small, tail=tail),
        out_shape=jax.ShapeDtypeStruct((depth, d_model, n_out * cb), BF16),
        grid=(depth, d_model // tk, n_out),
        in_specs=[blk(lambda l, k, c: (l, jnp.minimum(c, n_aligned), k)),
                  blk(lambda l, k, c: (l, jnp.clip(c + 1, n_aligned, n_in - 1), k)),
                  blk(lambda l, k, c: (l, n_aligned, k))],
        out_specs=pl.BlockSpec((None, tk, cb), lambda l, k, c: (l, k, c)),
        scratch_shapes=[pltpu.VMEM((cb, tk), F32)],
        compiler_params=_params(("parallel", "parallel", "arbitrary"), VMEM_LIMIT),
        name="in_proj_relayout",
    )(w_t, w_t, w_t)
    w1 = BRANCH_WIDTH
    col = {"qkv": 0, "dz": 3 * w1, "lx": 4 * w1, "lz": 5 * w1, "su": 6 * w1, "sz": 7 * w1,
           "mq": 8 * w1, "mz": 9 * w1, "glow": 10 * w1, "small": 10 * w1 + GATE_RANK}
    return w, col


def kernel(x, mem, norm_w, w_in, dn_conv_w, dn_a_log, dn_dt_bias, dn_norm_w, lru_conv_w, lru_conv_b, lru_w_r, lru_b_r, lru_w_i, lru_b_i, lru_lambda, ssm_log_dt, ssm_a_re, ssm_a_im, ssm_b_re, ssm_b_im, ssm_c_re, ssm_c_im, ssm_d, ssm_w_glu, ssm_b_glu, mem_norm_w, w_kv, w_gate, b_gate, w_branch, w_out, final_norm_w):
    batch, seq, d_model = x.shape
    mem_len = mem.shape[1]
    depth = w_in.shape[0]
    xf = x.reshape(batch * seq, d_model)
    memf = mem.reshape(batch * mem_len, d_model)
    w_cat, col = _in_proj_layout(w_in)
    w_gate_b = w_gate.astype(BF16)
    w_branch_b = w_branch.astype(BF16)
    w_out_b = w_out.astype(BF16)
    w_glu_b = ssm_w_glu.astype(BF16)
    w_ri = jnp.concatenate([lru_w_r, lru_w_i], axis=-1).astype(BF16)
    h = _rmsnorm(xf, norm_w[0], BF16)
    for l in range(depth):
        proj = _matmul(h, w_cat, l, F32)
        o_a = _deltanet(proj, dn_conv_w[l], dn_a_log[l], dn_dt_bias[l], dn_norm_w[l], batch, seq, col)
        o_b = _rglru(proj, lru_conv_w[l], lru_conv_b[l], w_ri, l, lru_b_r[l], lru_b_i[l],
                     lru_lambda[l], batch, seq, col)
        o_c = _s5(proj, ssm_log_dt[l], ssm_a_re[l], ssm_a_im[l], ssm_b_re[l], ssm_b_im[l],
                  ssm_c_re[l], ssm_c_im[l], ssm_d[l].reshape(-1), w_glu_b, l, ssm_b_glu[l], batch, seq, col)
        m_n = _rmsnorm(memf, mem_norm_w[l], BF16)
        kv = _matmul(m_n, w_kv, l, BF16)
        o_d = _mem_attn(proj, kv, batch, seq, mem_len, col)
        merged = _merge(proj, (o_a, o_b, o_c, o_d), w_gate_b, l, b_gate[l], w_branch_b, col)
        last = l == depth - 1
        next_norm_w = final_norm_w if last else norm_w[l + 1]
        xf, h = _out_proj_norm(merged, w_out_b, l, xf, next_norm_w, write_x=not last,
                               h_dtype=F32 if last else BF16)
    return h.reshape(batch, seq, d_model)
```
